```python
import math
import jax
import jax.numpy as jnp
from jax import lax
import numpy as np

D_MODEL = 2048
BATCH = 2
SEQ = 8192
DEPTH = 2

N_EVEN = (DEPTH + 1) // 2
N_ODD = DEPTH // 2

ROPE_THETA = 10000.0
NORM_EPS = 1e-6

DA_HEADS = 8
DA_DIM = 64
DA_VDIM = 2 * DA_DIM
DA_QBLK = 128

SW_HEADS = 16
SW_KV = 4
SW_DIM = 64
SW_WINDOW = 128

EVEN_IN = 3 * DA_HEADS * DA_VDIM + (SW_HEADS + 2 * SW_KV) * SW_DIM
EVEN_OUT = DA_HEADS * DA_VDIM + SW_HEADS * SW_DIM

NSA_HEADS = 16
NSA_KV = 2
NSA_DIM = 128
CMP_LEN = 32
CMP_STRIDE = 16
CMP_HIDDEN = 256
SLC_LEN = 64
SLC_TOPN = 16
NSA_WINDOW = 512
NSA_QBLK = 64

ODD_IN = NSA_HEADS * NSA_DIM + 6 * NSA_KV * NSA_DIM + 3 * NSA_HEADS
ODD_OUT = NSA_HEADS * NSA_DIM

PEER_HEADS = 8
PEER_NKEYS = 128
PEER_EXPERTS = PEER_NKEYS * PEER_NKEYS
PEER_TOPK = 16
PEER_HALF = 128
PEER_QDIM = 2 * PEER_HALF
PEER_CHUNK = 128

kernel_name = 'hybrid_diffattn_swa_nsa_peer_block'


def rmsnorm(x, g):
    xf = x.astype(jnp.float32)
    y = xf * lax.rsqrt(jnp.mean(xf * xf, axis=-1, keepdims=True) + NORM_EPS)
    return (y * g.astype(jnp.float32)).astype(x.dtype)


def rope_tables(pos, dim):
    inv = jnp.power(ROPE_THETA, -jnp.arange(0, dim, 2, dtype=jnp.float32) / dim)
    ang = pos.astype(jnp.float32)[:, None] * inv[None, :]
    return jnp.cos(ang), jnp.sin(ang)


def apply_rope(x, cos, sin):
    d2 = x.shape[-1] // 2
    shp = (1, cos.shape[0]) + (1,) * (x.ndim - 3) + (d2,)
    c = cos.reshape(shp).astype(x.dtype)
    s = sin.reshape(shp).astype(x.dtype)
    x1, x2 = x[..., :d2], x[..., d2:]
    return jnp.concatenate([x1 * c - x2 * s, x2 * c + x1 * s], axis=-1)


def masked_softmax(s, mask):
    s = jnp.where(mask, s.astype(jnp.float32), -jnp.inf)
    m = jnp.max(s, axis=-1, keepdims=True)
    m = jnp.where(jnp.isfinite(m), m, 0.0)
    p = jnp.exp(s - m)
    return p / jnp.maximum(jnp.sum(p, axis=-1, keepdims=True), 1e-30)


def ada_prenorm(x, c, w, b, g):
    mod = jax.nn.silu(c) @ w + b
    shift, scale, gate = jnp.split(mod[:, None, :], 3, axis=-1)
    h = rmsnorm(x, g) * (1 + scale) + shift
    return h, gate


def diff_attention(q, k, v, lam, lam_init, subln_g, cos, sin):
    B, S = q.shape[0], q.shape[1]
    q = apply_rope(q, cos, sin)
    k = apply_rope(k, cos, sin)
    nb = S // DA_QBLK
    qb = q.reshape(B, nb, DA_QBLK, 2, DA_HEADS, DA_DIM).transpose(1, 0, 2, 3, 4, 5)
    kpos = jnp.arange(S)
    scale = DA_DIM ** -0.5

    def block(args):
        qblk, i = args
        s = jnp.einsum('bqmhd,bkmhd->bmhqk', qblk, k) * scale
        tpos = i * DA_QBLK + jnp.arange(DA_QBLK)
        mask = (kpos[None, :] <= tpos[:, None])[None, None, None]
        p = masked_softmax(s, mask)
        a = p[:, 0] - lam * p[:, 1]
        return jnp.einsum('bhqk,bkhd->bqhd', a.astype(v.dtype), v)

    o = lax.map(block, (qb, jnp.arange(nb)))
    o = o.transpose(1, 0, 2, 3, 4).reshape(B, S, DA_HEADS, DA_VDIM)
    o = rmsnorm(o, subln_g) * (1.0 - lam_init)
    return o.reshape(B, S, DA_HEADS * DA_VDIM)


def sliding_window_sink_attention(q, k, v, sinks, cos, sin):
    B, S = q.shape[0], q.shape[1]
    G = SW_HEADS // SW_KV
    W = SW_WINDOW
    q = apply_rope(q, cos, sin)
    k = apply_rope(k, cos, sin)
    nb = S // W
    qb = q.reshape(B, nb, W, SW_KV, G, SW_DIM)

    def band(t):
        tb = t.reshape(B, nb, W, SW_KV, SW_DIM)
        prev = jnp.pad(tb, ((0, 0), (1, 0), (0, 0), (0, 0), (0, 0)))[:, :-1]
        return jnp.concatenate([prev, tb], axis=2)

    kb, vb = band(k), band(v)
    s = jnp.einsum('bnqhgd,bnkhd->bnhgqk', qb, kb).astype(jnp.float32) * (SW_DIM ** -0.5)
    qi = jnp.arange(W)[:, None] + W
    ki = jnp.arange(2 * W)[None, :]
    rel = qi - ki
    kabs = jnp.arange(nb)[:, None, None] * W - W + ki[None]
    mask = (rel >= 0)[None] & (rel < W)[None] & (kabs >= 0)
    s = jnp.where(mask[None, :, None, None], s, -jnp.inf)
    sink = sinks.astype(jnp.float32).reshape(SW_KV, G)[None, None, :, :, None, None]
    m = jnp.maximum(jnp.max(s, axis=-1, keepdims=True), sink)
    p = jnp.exp(s - m)
    p = p / (jnp.sum(p, axis=-1, keepdims=True) + jnp.exp(sink - m))
    o = jnp.einsum('bnhgqk,bnkhd->bnqhgd', p.astype(v.dtype), vb)
    return o.reshape(B, S, SW_HEADS * SW_DIM)


def even_mixer(h, w_in, w_out, lam_vecs, subln_g, sinks, lam_init, cos, sin):
    B, S, _ = h.shape
    da_qk = 2 * DA_HEADS * DA_DIM
    da_v = DA_HEADS * DA_VDIM
    sizes = [da_qk, da_qk, da_v, SW_HEADS * SW_DIM, SW_KV * SW_DIM]
    qa, ka, va, qs, ks, vs = jnp.split(h @ w_in, [int(n) for n in np.cumsum(sizes)], axis=-1)
    lv = lam_vecs.astype(jnp.float32)
    lam = jnp.exp(jnp.sum(lv[0] * lv[1])) - jnp.exp(jnp.sum(lv[2] * lv[3])) + lam_init
    oa = diff_attention(qa.reshape(B, S, 2, DA_HEADS, DA_DIM),
                        ka.reshape(B, S, 2, DA_HEADS, DA_DIM),
                        va.reshape(B, S, DA_HEADS, DA_VDIM),
                        lam, lam_init, subln_g, cos, sin)
    ob = sliding_window_sink_attention(qs.reshape(B, S, SW_HEADS, SW_DIM),
                                       ks.reshape(B, S, SW_KV, SW_DIM),
                                       vs.reshape(B, S, SW_KV, SW_DIM),
                                       sinks, cos, sin)
    return jnp.concatenate([oa, ob], axis=-1) @ w_out


def compress_blocks(t, pos_emb, w1, w2):
    B, S, H, d = t.shape
    r = CMP_LEN // CMP_STRIDE
    nch = S // CMP_STRIDE
    ch = t.reshape(B, nch, CMP_STRIDE, H, d)
    blocks = jnp.concatenate([ch[:, j:nch - r + 1 + j] for j in range(r)], axis=2)
    blocks = blocks + pos_emb[None, None, :, None, :]
    nc = blocks.shape[1]
    flat = blocks.transpose(0, 1, 3, 2, 4).reshape(B, nc, H, CMP_LEN * d)
    return jax.nn.gelu(flat @ w1) @ w2


def nsa_mixer(h, w_in, w_out, cmp_pos, cmp_w1, cmp_w2, cos, sin):
    B, S, _ = h.shape
    G = NSA_HEADS // NSA_KV
    d = NSA_DIM
    kvw = NSA_KV * d
    sizes = [NSA_HEADS * d] + [kvw] * 6
    q, kc, vc, ksl, vsl, kw, vw, gates = jnp.split(h @ w_in, [int(n) for n in np.cumsum(sizes)], axis=-1)
    q = apply_rope(q.reshape(B, S, NSA_HEADS, d), cos, sin)
    kv = lambda t: t.reshape(B, S, NSA_KV, d)
    ksl = apply_rope(kv(ksl), cos, sin)
    kw = apply_rope(kv(kw), cos, sin)
    vsl, vw = kv(vsl), kv(vw)
    kcmp = compress_blocks(kv(kc), cmp_pos[0], cmp_w1[0], cmp_w2[0])
    vcmp = compress_blocks(kv(vc), cmp_pos[1], cmp_w1[1], cmp_w2[1])
    nc = kcmp.shape[1]
    cmp_end = jnp.arange(nc) * CMP_STRIDE + CMP_LEN - 1
    kcmp = apply_rope(kcmp, cos[cmp_end], sin[cmp_end])
    ns = S // SLC_LEN
    topn = min(SLC_TOPN, ns)
    cst = jnp.arange(nc)[:, None] * CMP_STRIDE
    sst = jnp.arange(ns)[None, :] * SLC_LEN
    overlap = jnp.clip(jnp.minimum(cst + CMP_LEN, sst + SLC_LEN) - jnp.maximum(cst, sst), 0, None)
    overlap = overlap.astype(jnp.float32) / CMP_LEN
    ksb = ksl.reshape(B, ns, SLC_LEN, NSA_KV, d).transpose(0, 3, 1, 2, 4)
    vsb = vsl.reshape(B, ns, SLC_LEN, NSA_KV, d).transpose(0, 3, 1, 2, 4)
    pad = ((0, 0), (NSA_WINDOW, 0), (0, 0), (0, 0))
    kwp, vwp = jnp.pad(kw, pad), jnp.pad(vw, pad)
    nb = S // NSA_QBLK
    qb = q.reshape(B, nb, NSA_QBLK, NSA_KV, G, d).transpose(1, 0, 2, 3, 4, 5)
    gb = jax.nn.sigmoid(gates).reshape(B, nb, NSA_QBLK, 3, NSA_KV, G).transpose(1, 0, 2, 3, 4, 5)
    scale = d ** -0.5
    bidx = jnp.arange(B)[:, None, None, None]
    hidx = jnp.arange(NSA_KV)[None, None, :, None]
    blk = jnp.arange(ns)
    span = NSA_WINDOW + NSA_QBLK

    def block(args):
        qblk, gblk, i = args
        t0 = i * NSA_QBLK
        tpos = t0 + jnp.arange(NSA_QBLK)
        s = jnp.einsum('bthgd,bchd->bthgc', qblk, kcmp) * scale
        mask_c = (cmp_end[None, :] <= tpos[:, None])[None, :, None, None, :]
        p_c = masked_softmax(s, mask_c)
        o_c = jnp.einsum('bthgc,bchd->bthgd', p_c.astype(vcmp.dtype), vcmp)
        imp = jnp.einsum('bthgc,cs->bths', p_c, overlap)
        cur = tpos // SLC_LEN
        future = blk[None, :] > cur[:, None]
        forced = (blk[None, :] == 0) | (blk[None, :] == cur[:, None]) | (blk[None, :] == cur[:, None] - 1)
        imp = jnp.where(forced[None, :, None, :], jnp.inf, imp)
        imp = jnp.where(future[None, :, None, :], -jnp.inf, imp)
        _, idx = lax.top_k(imp, topn)
        k_sel = ksb[bidx, hidx, idx]
        v_sel = vsb[bidx, hidx, idx]
        kpos = idx[..., None] * SLC_LEN + jnp.arange(SLC_LEN)
        s = jnp.einsum('bthgd,bthnld->bthgnl', qblk, k_sel) * scale
        s = s.reshape(B, NSA_QBLK, NSA_KV, G, topn * SLC_LEN)
        mask_s = (kpos <= tpos[None, :, None, None, None]).reshape(B, NSA_QBLK, NSA_KV, 1, topn * SLC_LEN)
        p_s = masked_softmax(s, mask_s)
        o_s = jnp.einsum('bthgm,bthmd->bthgd', p_s.astype(v_sel.dtype),
                         v_sel.reshape(B, NSA_QBLK, NSA_KV, topn * SLC_LEN, d))
        kwin = lax.dynamic_slice_in_dim(kwp, t0, span, axis=1)
        vwin = lax.dynamic_slice_in_dim(vwp, t0, span, axis=1)
        wpos = t0 - NSA_WINDOW + jnp.arange(span)
        rel = tpos[:, None] - wpos[None, :]
        mask_w = ((rel >= 0) & (rel < NSA_WINDOW) & (wpos[None, :] >= 0))[None, :, None, None, :]
        s = jnp.einsum('bthgd,bkhd->bthgk', qblk, kwin) * scale
        p_w = masked_softmax(s, mask_w)
        o_w = jnp.einsum('bthgk,bkhd->bthgd', p_w.astype(vwin.dtype), vwin)
        g = gblk[..., None].astype(o_c.dtype)
        return g[:, :, 0] * o_c + g[:, :, 1] * o_s + g[:, :, 2] * o_w

    o = lax.map(block, (qb, gb, jnp.arange(nb)))
    o = o.transpose(1, 0, 2, 3, 4, 5).reshape(B, S, NSA_HEADS * d)
    return o @ w_out


def peer_ffn(h, wq, subkeys, u_tab, v_tab):
    B, S, D = h.shape
    T = B * S
    ht = h.reshape(T // PEER_CHUNK, PEER_CHUNK, D)
    K = PEER_TOPK

    def chunk(hc):
        q = (hc @ wq).reshape(PEER_CHUNK, PEER_HEADS, 2, PEER_HALF)
        s = jnp.einsum('thpd,hpkd->thpk', q, subkeys).astype(jnp.float32)
        sv, si = lax.top_k(s, K)
        cand = (sv[:, :, 0, :, None] + sv[:, :, 1, None, :]).reshape(PEER_CHUNK, PEER_HEADS, K * K)
        cidx = (si[:, :, 0, :, None] * PEER_NKEYS + si[:, :, 1, None, :]).reshape(PEER_CHUNK, PEER_HEADS, K * K)
        best, pos = lax.top_k(cand, K)
        eidx = jnp.take_along_axis(cidx, pos, axis=-1)
        g = jax.nn.softmax(best, axis=-1)
        u = u_tab[eidx]
        a = jax.nn.gelu(jnp.einsum('td,thkd->thk', hc, u).astype(jnp.float32))
        coef = (g * a).astype(hc.dtype)
        return jnp.einsum('thk,thkd->td', coef, v_tab[eidx])

    return lax.map(chunk, ht).reshape(B, S, D)


def setup_inputs(seed: int = 0) -> dict:
    key = jax.random.key(seed)
    ks = jax.random.split(key, 20)
    nrm = lambda k, shp, sc: jax.random.normal(k, shp, jnp.float32) * sc
    D = D_MODEL
    return {
        'x': nrm(ks[0], (BATCH, SEQ, D), 1.0),
        'c': nrm(ks[1], (BATCH, D), 1.0),
        'ada_w': nrm(ks[2], (DEPTH, 2, D, 3 * D), 0.5 * D ** -0.5),
        'ada_b': nrm(ks[3], (DEPTH, 2, 3 * D), 0.02),
        'norm_g': 1.0 + nrm(ks[4], (DEPTH, 2, D), 0.05),
        'even_w_in': nrm(ks[5], (N_EVEN, D, EVEN_IN), D ** -0.5),
        'even_w_out': nrm(ks[6], (N_EVEN, EVEN_OUT, D), EVEN_OUT ** -0.5),
        'da_lambda': nrm(ks[7], (N_EVEN, 4, DA_DIM), 0.1),
        'da_subln': 1.0 + nrm(ks[8], (N_EVEN, DA_VDIM), 0.05),
        'sw_sinks': nrm(ks[9], (N_EVEN, SW_HEADS), 0.5),
        'odd_w_in': nrm(ks[10], (N_ODD, D, ODD_IN), D ** -0.5),
        'odd_w_out': nrm(ks[11], (N_ODD, ODD_OUT, D), ODD_OUT ** -0.5),
        'nsa_cmp_pos': nrm(ks[12], (N_ODD, 2, CMP_LEN, NSA_DIM), 0.5),
        'nsa_cmp_w1': nrm(ks[13], (N_ODD, 2, CMP_LEN * NSA_DIM, CMP_HIDDEN), (CMP_LEN * NSA_DIM) ** -0.5),
        'nsa_cmp_w2': nrm(ks[14], (N_ODD, 2, CMP_HIDDEN, NSA_DIM), CMP_HIDDEN ** -0.5),
        'peer_wq': nrm(ks[15], (DEPTH, D, PEER_HEADS * PEER_QDIM), D ** -0.5),
        'peer_subkeys': nrm(ks[16], (DEPTH, PEER_HEADS, 2, PEER_NKEYS, PEER_HALF), PEER_HALF ** -0.5),
        'peer_u': nrm(ks[17], (DEPTH, PEER_EXPERTS, D), D ** -0.5),
        'peer_v': nrm(ks[18], (DEPTH, PEER_EXPERTS, D), 1.0),
        'final_g': 1.0 + nrm(ks[19], (D,), 0.05),
    }


def reference(x, c, ada_w, ada_b, norm_g, even_w_in, even_w_out, da_lambda, da_subln, sw_sinks,
              odd_w_in, odd_w_out, nsa_cmp_pos, nsa_cmp_w1, nsa_cmp_w2,
              peer_wq, peer_subkeys, peer_u, peer_v, final_g):
    S = x.shape[1]
    pos = jnp.arange(S)
    cos64, sin64 = rope_tables(pos, DA_DIM)
    cos128, sin128 = rope_tables(pos, NSA_DIM)
    for layer in range(DEPTH):
        j = layer // 2
        h, gate = ada_prenorm(x, c, ada_w[layer, 0], ada_b[layer, 0], norm_g[layer, 0])
        if layer % 2 == 0:
            lam_init = 0.8 - 0.6 * math.exp(-0.3 * layer)
            y = even_mixer(h, even_w_in[j], even_w_out[j], da_lambda[j], da_subln[j], sw_sinks[j],
                           lam_init, cos64, sin64)
        else:
            y = nsa_mixer(h, odd_w_in[j], odd_w_out[j], nsa_cmp_pos[j], nsa_cmp_w1[j], nsa_cmp_w2[j],
                          cos128, sin128)
        x = x + gate * y
        h, gate = ada_prenorm(x, c, ada_w[layer, 1], ada_b[layer, 1], norm_g[layer, 1])
        x = x + gate * peer_ffn(h, peer_wq[layer], peer_subkeys[layer], peer_u[layer], peer_v[layer])
    return rmsnorm(x, final_g)
```

```python
import functools
import math

import jax
import jax.numpy as jnp
import numpy as np
from jax import lax
from jax.experimental import pallas as pl
from jax.experimental.pallas import tpu as pltpu

F32 = jnp.float32
BF16 = jnp.bfloat16
NEG_INF = float("-inf")

D_MODEL = 2048
ROPE_THETA = 10000.0
NORM_EPS = 1e-6

DA_HEADS = 8
DA_DIM = 64
DA_VDIM = 128
SW_HEADS = 16
SW_KV = 4
SW_DIM = 64
SW_WINDOW = 128

NSA_HEADS = 16
NSA_KV = 2
NSA_G = NSA_HEADS // NSA_KV
NSA_DIM = 128
CMP_LEN = 32
CMP_STRIDE = 16
CMP_HIDDEN = 256
SLC_LEN = 64
SLC_TOPN = 16
NSA_WINDOW = 512

PEER_HEADS = 8
PEER_NKEYS = 128
PEER_TOPK = 16
PEER_HALF = 128

LANES = 128
V7X_VMEM_BYTES = 64 * 1024 * 1024

_NT = (((1,), (1,)), ((), ()))


def _cparams(sem, vmem_mb=48):
    assert vmem_mb * 1024 * 1024 < V7X_VMEM_BYTES
    return pltpu.CompilerParams(dimension_semantics=sem, vmem_limit_bytes=vmem_mb * 1024 * 1024)


def _ada_kernel(c_ref, w_ref, b_ref, o_ref):
    c = c_ref[...]
    sc = c * jax.nn.sigmoid(c)
    o_ref[0] = jnp.dot(sc, w_ref[0], preferred_element_type=F32,
                       precision=lax.Precision.HIGHEST) + b_ref[0]


def ada_mod(c, ada_w, ada_b):
    B, D = c.shape
    n = ada_w.shape[0] * ada_w.shape[1]
    w = ada_w.reshape(n, D, 3 * D)
    b = ada_b.reshape(n, 1, 3 * D)
    rows = 8
    cp = jnp.zeros((rows, D), F32).at[:B].set(c)
    tn = 512
    out = pl.pallas_call(
        _ada_kernel,
        grid=(n, 3 * D // tn),
        in_specs=[pl.BlockSpec((rows, D), lambda l, j: (0, 0)),
                  pl.BlockSpec((1, D, tn), lambda l, j: (l, 0, j)),
                  pl.BlockSpec((1, 1, tn), lambda l, j: (l, 0, j))],
        out_specs=pl.BlockSpec((1, rows, tn), lambda l, j: (l, 0, j)),
        out_shape=jax.ShapeDtypeStruct((n, rows, 3 * D), F32),
        compiler_params=_cparams(("arbitrary", "arbitrary")),
        name="ada_mod",
    )(cp, w, b)
    return out[:, :B]


def _nmm_kernel(x_ref, g_ref, sc_ref, sh_ref, w_ref, *rest, emit_h):
    if emit_h:
        y_ref, h_out_ref, hs_ref = rest
    else:
        y_ref, hs_ref = rest

    @pl.when(pl.program_id(1) == 0)
    def _():
        x = x_ref[...]
        ms = jnp.mean(x * x, axis=-1, keepdims=True)
        y = x * lax.rsqrt(ms + NORM_EPS) * g_ref[...]
        h = (y * (1.0 + sc_ref[0]) + sh_ref[0]).astype(BF16)
        hs_ref[...] = h
        if emit_h:
            h_out_ref[...] = h

    y_ref[...] = jnp.dot(hs_ref[...], w_ref[...], preferred_element_type=F32).astype(y_ref.dtype)


def norm_mod_matmul(x2, g, scale, shift, w, S, *, emit_h=False, out_dtype=F32, tm=1024, tn=512):
    T, D = x2.shape
    N = w.shape[1]
    tm = min(tm, S)
    assert S % tm == 0 and N % tn == 0
    per = S // tm
    B = T // S
    out_shape = [jax.ShapeDtypeStruct((T, N), out_dtype)]
    out_specs = [pl.BlockSpec((tm, tn), lambda i, j: (i, j))]
    if emit_h:
        out_shape.append(jax.ShapeDtypeStruct((T, D), BF16))
        out_specs.append(pl.BlockSpec((tm, D), lambda i, j: (i, 0)))
    res = pl.pallas_call(
        functools.partial(_nmm_kernel, emit_h=emit_h),
        grid=(T // tm, N // tn),
        in_specs=[pl.BlockSpec((tm, D), lambda i, j: (i, 0)),
                  pl.BlockSpec((1, D), lambda i, j: (0, 0)),
                  pl.BlockSpec((1, 1, D), lambda i, j: (i // per, 0, 0)),
                  pl.BlockSpec((1, 1, D), lambda i, j: (i // per, 0, 0)),
                  pl.BlockSpec((D, tn), lambda i, j: (0, j))],
        out_specs=out_specs,
        out_shape=out_shape,
        scratch_shapes=[pltpu.VMEM((tm, D), BF16)],
        compiler_params=_cparams(("arbitrary", "arbitrary")),
        name="norm_mod_matmul",
    )(x2, g.reshape(1, D), scale.reshape(B, 1, D), shift.reshape(B, 1, D), w)
    return res if emit_h else res[0]


def _mmres_kernel(a_ref, w_ref, x_ref, gate_ref, o_ref):
    y = jnp.dot(a_ref[...], w_ref[...], preferred_element_type=F32)
    o_ref[...] = x_ref[...] + gate_ref[0] * y


def matmul_residual(a, w, x2, gate, S, *, tm=1024, tn=512):
    T, K = a.shape
    N = w.shape[1]
    tm = min(tm, S)
    per = S // tm
    B = T // S
    return pl.pallas_call(
        _mmres_kernel,
        grid=(T // tm, N // tn),
        in_specs=[pl.BlockSpec((tm, K), lambda i, j: (i, 0)),
                  pl.BlockSpec((K, tn), lambda i, j: (0, j)),
                  pl.BlockSpec((tm, tn), lambda i, j: (i, j)),
                  pl.BlockSpec((1, 1, tn), lambda i, j: (i // per, 0, j))],
        out_specs=pl.BlockSpec((tm, tn), lambda i, j: (i, j)),
        out_shape=jax.ShapeDtypeStruct((T, N), F32),
        compiler_params=_cparams(("arbitrary", "arbitrary")),
        name="matmul_residual",
    )(a, w, x2, gate.reshape(B, 1, N))


def _rope_block(v, cos, sin, hd):
    if hd == 64:
        lane = lax.broadcasted_iota(jnp.int32, v.shape, 1)
        lo = (lane % 64) < 32
        partner = jnp.where(lo, pltpu.roll(v, 96, 1), pltpu.roll(v, 32, 1))
    else:
        partner = pltpu.roll(v, 64, 1)
    return v * cos + partner * sin


def _rope_kernel(y_ref, cos_ref, sin_ref, o_ref, *, hd):
    cos = cos_ref[...]
    sin = sin_ref[...]
    for k in range(o_ref.shape[1] // LANES):
        sl = slice(k * LANES, (k + 1) * LANES)
        o_ref[:, sl] = _rope_block(y_ref[:, sl], cos, sin, hd).astype(o_ref.dtype)


def _cast_kernel(y_ref, o_ref):
    o_ref[...] = y_ref[...].astype(o_ref.dtype)


def rope_cols(y, col0, ncols, cos, sin, hd, S, *, tm=512, cw=512):
    T = y.shape[0]
    tm = min(tm, S)
    per = S // tm
    assert col0 % cw == 0 and ncols % cw == 0
    c0 = col0 // cw
    return pl.pallas_call(
        functools.partial(_rope_kernel, hd=hd),
        grid=(T // tm, ncols // cw),
        in_specs=[pl.BlockSpec((tm, cw), lambda i, j: (i, c0 + j)),
                  pl.BlockSpec((tm, LANES), lambda i, j: (i % per, 0)),
                  pl.BlockSpec((tm, LANES), lambda i, j: (i % per, 0))],
        out_specs=pl.BlockSpec((tm, cw), lambda i, j: (i, j)),
        out_shape=jax.ShapeDtypeStruct((T, ncols), BF16),
        compiler_params=_cparams(("arbitrary", "arbitrary")),
        name="rope_cols",
    )(y, cos, sin)


def cast_cols(y, col0, ncols, dtype, S, *, tm=512, cw=512):
    T = y.shape[0]
    tm = min(tm, S)
    cw = min(cw, ncols)
    assert col0 % cw == 0 and ncols % cw == 0
    c0 = col0 // cw
    return pl.pallas_call(
        _cast_kernel,
        grid=(T // tm, ncols // cw),
        in_specs=[pl.BlockSpec((tm, cw), lambda i, j: (i, c0 + j))],
        out_specs=pl.BlockSpec((tm, cw), lambda i, j: (i, j)),
        out_shape=jax.ShapeDtypeStruct((T, ncols), dtype),
        compiler_params=_cparams(("arbitrary", "arbitrary")),
        name="cast_cols",
    )(y)


def _rope_tables(pos, hd):
    inv = jnp.power(ROPE_THETA, -jnp.arange(0, hd, 2, dtype=F32) / hd)
    ang = pos.astype(F32)[:, None] * inv[None, :]
    cos, sin = jnp.cos(ang), jnp.sin(ang)
    reps = LANES // hd
    cos_l = jnp.tile(jnp.concatenate([cos, cos], axis=1), (1, reps))
    sin_l = jnp.tile(jnp.concatenate([-sin, sin], axis=1), (1, reps))
    return cos_l, sin_l


def _split_halves(q):
    lane = lax.broadcasted_iota(jnp.int32, q.shape, 1)
    zero = jnp.zeros_like(q)
    return jnp.concatenate([jnp.where(lane < 64, q, zero), jnp.where(lane >= 64, q, zero)], axis=0)


def _diff_kernel(q_ref, k_ref, v_ref, lv_ref, sg_ref, o_ref, *, tq, lam_init):
    i = pl.program_id(2)
    qq = _split_halves(q_ref[...])
    rows = 2 * tq
    tk = tq

    def step(c, carry, masked):
        m, l, acc = carry
        k0 = pl.multiple_of(c * tk, tk)
        kc = k_ref[pl.ds(k0, tk), :]
        vc = v_ref[pl.ds(k0, tk), :]
        s = lax.dot_general(qq, kc, _NT, preferred_element_type=F32)
        if masked:
            r = lax.broadcasted_iota(jnp.int32, (rows, tk), 0) % tq
            cc = lax.broadcasted_iota(jnp.int32, (rows, tk), 1)
            s = jnp.where(cc <= r, s, NEG_INF)
        m_new = jnp.maximum(m, jnp.max(s, axis=-1, keepdims=True))
        alpha = jnp.exp(m - m_new)
        p = jnp.exp(s - m_new)
        l = alpha * l + jnp.sum(p, axis=-1, keepdims=True)
        acc = alpha * acc + jnp.dot(p.astype(BF16), vc, preferred_element_type=F32)
        return m_new, l, acc

    init = (jnp.full((rows, 1), NEG_INF, F32), jnp.zeros((rows, 1), F32), jnp.zeros((rows, LANES), F32))
    carry = lax.fori_loop(0, i, functools.partial(step, masked=False), init)
    m, l, acc = step(i, carry, True)
    o = acc / l
    lv = lv_ref[...]
    lam = (jnp.exp(jnp.sum(lv[0:1] * lv[1:2], axis=-1, keepdims=True))
           - jnp.exp(jnp.sum(lv[2:3] * lv[3:4], axis=-1, keepdims=True)) + lam_init)
    o = o[:tq] - lam * o[tq:]
    o = o * lax.rsqrt(jnp.mean(o * o, axis=-1, keepdims=True) + NORM_EPS) * sg_ref[...]
    o_ref[...] = (o * (1.0 - lam_init)).astype(o_ref.dtype)


def diff_attention(qk, v, lam_vecs, subln_g, B, S, lam_init, *, q_col0, k_col0, v_col0, tq=256):
    T = B * S
    tq = min(tq, S)
    nq = S // tq
    H = DA_HEADS
    qb, kb, vb = q_col0 // LANES, k_col0 // LANES, v_col0 // LANES
    return pl.pallas_call(
        functools.partial(_diff_kernel, tq=tq, lam_init=lam_init),
        grid=(B, H, nq),
        in_specs=[pl.BlockSpec((tq, LANES), lambda b, h, i: (b * nq + i, qb + h)),
                  pl.BlockSpec((S, LANES), lambda b, h, i: (b, kb + h)),
                  pl.BlockSpec((S, LANES), lambda b, h, i: (b, vb + h)),
                  pl.BlockSpec((4, DA_DIM), lambda b, h, i: (0, 0)),
                  pl.BlockSpec((1, DA_VDIM), lambda b, h, i: (0, 0))],
        out_specs=pl.BlockSpec((tq, LANES), lambda b, h, i: (b * nq + i, h)),
        out_shape=jax.ShapeDtypeStruct((T, H * DA_VDIM), BF16),
        compiler_params=_cparams(("arbitrary", "arbitrary", "arbitrary")),
        name="diff_attention",
    )(qk, qk, v, lam_vecs.astype(F32), subln_g.reshape(1, DA_VDIM).astype(F32))


def _swa_kernel(sink_ref, q_ref, k_ref, v_ref, o_ref, *, tq, window):
    pr = pl.program_id(1)
    i = pl.program_id(2)
    q0 = i * tq
    span = tq + window
    ks = pl.multiple_of(jnp.maximum(q0 - window, 0), window)
    qq = _split_halves(q_ref[...])
    rows = 2 * tq
    kc = k_ref[pl.ds(ks, span), :]
    vc = v_ref[pl.ds(ks, span), :]
    s = lax.dot_general(qq, kc, _NT, preferred_element_type=F32)
    r = lax.broadcasted_iota(jnp.int32, (rows, span), 0) % tq
    cc = lax.broadcasted_iota(jnp.int32, (rows, span), 1)
    rel = (q0 - ks) + r - cc
    s = jnp.where((rel >= 0) & (rel < window), s, NEG_INF)
    rr = lax.broadcasted_iota(jnp.int32, (rows, 1), 0)
    sink = jnp.where(rr < tq, sink_ref[2 * pr], sink_ref[2 * pr + 1])
    m = jnp.maximum(jnp.max(s, axis=-1, keepdims=True), sink)
    p = jnp.exp(s - m)
    den = jnp.sum(p, axis=-1, keepdims=True) + jnp.exp(sink - m)
    o2 = jnp.dot(p.astype(BF16), vc, preferred_element_type=F32) / den
    lane = lax.broadcasted_iota(jnp.int32, (tq, LANES), 1)
    o_ref[...] = jnp.where(lane < 64, o2[:tq], o2[tq:]).astype(o_ref.dtype)


def swa_attention(qk, v, sinks, B, S, *, q_col0, k_col0, v_col0, tq=256):
    T = B * S
    tq = min(tq, S)
    nq = S // tq
    npair = SW_HEADS // 2
    ppk = (SW_HEADS // SW_KV) // 2
    qb, kb, vb = q_col0 // LANES, k_col0 // LANES, v_col0 // LANES
    return pl.pallas_call(
        functools.partial(_swa_kernel, tq=tq, window=SW_WINDOW),
        grid=(B, npair, nq),
        in_specs=[pl.BlockSpec(memory_space=pltpu.SMEM),
                  pl.BlockSpec((tq, LANES), lambda b, p, i: (b * nq + i, qb + p)),
                  pl.BlockSpec((S, LANES), lambda b, p, i: (b, kb + p // ppk)),
                  pl.BlockSpec((S, LANES), lambda b, p, i: (b, vb + p // ppk))],
        out_specs=pl.BlockSpec((tq, LANES), lambda b, p, i: (b * nq + i, p)),
        out_shape=jax.ShapeDtypeStruct((T, SW_HEADS * SW_DIM), BF16),
        compiler_params=_cparams(("arbitrary", "arbitrary", "arbitrary")),
        name="swa_attention",
    )(sinks.astype(F32), qk, qk, v)


def _compress_kernel(x_ref, pos_ref, w1a_ref, w1b_ref, w2_ref, cos_ref, sin_ref, o_ref, acc_a, acc_b):
    tok = pl.program_id(1)
    ntok = pl.num_programs(1)

    @pl.when(tok == 0)
    def _():
        acc_a[...] = jnp.zeros_like(acc_a)
        acc_b[...] = jnp.zeros_like(acc_b)

    for kv in range(2):
        pa = pos_ref[kv, pl.ds(tok, 1), :]
        pb = pos_ref[kv, pl.ds(CMP_STRIDE + tok, 1), :]
        for h in range(NSA_KV):
            idx = kv * NSA_KV + h
            xs = x_ref[0, :, idx * LANES:(idx + 1) * LANES]
            acc_a[idx] += jnp.dot((xs + pa).astype(BF16), w1a_ref[kv], preferred_element_type=F32)
            acc_b[idx] += jnp.dot((xs + pb).astype(BF16), w1b_ref[kv], preferred_element_type=F32)

    @pl.when(tok == ntok - 1)
    def _():
        ncp = acc_a.shape[1]
        for kv in range(2):
            for h in range(NSA_KV):
                idx = kv * NSA_KV + h
                hid = acc_a[idx] + pltpu.roll(acc_b[idx], ncp - 1, 0)
                out = jnp.dot(jax.nn.gelu(hid).astype(BF16), w2_ref[kv], preferred_element_type=F32)
                if kv == 0:
                    out = _rope_block(out, cos_ref[...], sin_ref[...], NSA_DIM)
                o_ref[0, idx] = out.astype(o_ref.dtype)


def nsa_compress(xc, cmp_pos, cmp_w1, cmp_w2, cos_c, sin_c, B, S):
    ncp = S // CMP_STRIDE
    width = 2 * NSA_KV * NSA_DIM
    x3 = xc.reshape(B, ncp, CMP_STRIDE * width)
    half = CMP_STRIDE * NSA_DIM
    w1 = cmp_w1.astype(BF16)
    return pl.pallas_call(
        _compress_kernel,
        grid=(B, CMP_STRIDE),
        in_specs=[pl.BlockSpec((1, ncp, width), lambda b, t: (b, 0, t)),
                  pl.BlockSpec((2, CMP_LEN, NSA_DIM), lambda b, t: (0, 0, 0)),
                  pl.BlockSpec((2, NSA_DIM, CMP_HIDDEN), lambda b, t: (0, t, 0)),
                  pl.BlockSpec((2, NSA_DIM, CMP_HIDDEN), lambda b, t: (0, CMP_STRIDE + t, 0)),
                  pl.BlockSpec((2, CMP_HIDDEN, NSA_DIM), lambda b, t: (0, 0, 0)),
                  pl.BlockSpec((ncp, LANES), lambda b, t: (0, 0)),
                  pl.BlockSpec((ncp, LANES), lambda b, t: (0, 0))],
        out_specs=pl.BlockSpec((1, 2 * NSA_KV, ncp, NSA_DIM), lambda b, t: (b, 0, 0, 0)),
        out_shape=jax.ShapeDtypeStruct((B, 2 * NSA_KV, ncp, NSA_DIM), BF16),
        scratch_shapes=[pltpu.VMEM((2 * NSA_KV, ncp, CMP_HIDDEN), F32),
                        pltpu.VMEM((2 * NSA_KV, ncp, CMP_HIDDEN), F32)],
        compiler_params=_cparams(("arbitrary", "arbitrary")),
        name="nsa_compress",
    )(x3, cmp_pos.astype(F32), w1, w1, cmp_w2.astype(BF16), cos_c, sin_c)


def _stack_heads(q_ref, tq):
    return jnp.concatenate([q_ref[:, g * LANES:(g + 1) * LANES] for g in range(NSA_G)], axis=0)


def _nsa_cmp_kernel(q_ref, kc_ref, vc_ref, ovt_ref, o_ref, sel_ref, imp_scr, *, tq, scale):
    i = pl.program_id(2)
    q0 = i * tq
    rows = NSA_G * tq
    ncp = kc_ref.shape[2]
    ns = ovt_ref.shape[0]
    qq = _stack_heads(q_ref, tq)
    s = lax.dot_general(qq, kc_ref[0, 0], _NT, preferred_element_type=F32) * scale
    tpos = q0 + lax.broadcasted_iota(jnp.int32, (rows, ncp), 0) % tq
    cend = lax.broadcasted_iota(jnp.int32, (rows, ncp), 1) * CMP_STRIDE + (CMP_LEN - 1)
    s = jnp.where(cend <= tpos, s, NEG_INF)
    m = jnp.max(s, axis=-1, keepdims=True)
    m = jnp.where(m == NEG_INF, 0.0, m)
    p = jnp.exp(s - m)
    p = p / jnp.maximum(jnp.sum(p, axis=-1, keepdims=True), 1e-30)
    o = jnp.dot(p.astype(BF16), vc_ref[0, 0], preferred_element_type=F32)
    for g in range(NSA_G):
        o_ref[:, g * LANES:(g + 1) * LANES] = o[g * tq:(g + 1) * tq]
    psum = p[0:tq]
    for g in range(1, NSA_G):
        psum = psum + p[g * tq:(g + 1) * tq]
    p_hi = psum.astype(BF16)
    p_lo = (psum - p_hi.astype(F32)).astype(BF16)
    ovt = ovt_ref[...]
    imp = (lax.dot_general(ovt, p_hi, _NT, preferred_element_type=F32)
           + lax.dot_general(ovt, p_lo, _NT, preferred_element_type=F32))
    blk = lax.broadcasted_iota(jnp.int32, (ns, tq), 0)
    cur = (q0 + lax.broadcasted_iota(jnp.int32, (ns, tq), 1)) // SLC_LEN
    future = blk > cur
    forced = (blk == 0) | (blk == cur) | (blk == cur - 1)
    imp = jnp.where(forced, jnp.inf, imp)
    imp = jnp.where(future, NEG_INF, imp)
    imp_scr[...] = imp

    def count(sp, cnt):
        row = imp_scr[pl.ds(sp, 1), :]
        beats = (row > imp) | ((row == imp) & (sp < blk))
        return cnt + beats.astype(F32)

    cnt = lax.fori_loop(0, ns, count, jnp.zeros((ns, tq), F32))
    sel_t = jnp.where((cnt < float(SLC_TOPN)) & jnp.logical_not(future), 1.0, 0.0)
    sel_ref[0, 0] = sel_t.T.astype(sel_ref.dtype)


def nsa_compressed(q, kvc, ovt, B, S, *, tq=128):
    T = B * S
    nq = S // tq
    ncp = S // CMP_STRIDE
    ns = S // SLC_LEN
    gw = NSA_G * NSA_DIM
    return pl.pallas_call(
        functools.partial(_nsa_cmp_kernel, tq=tq, scale=NSA_DIM ** -0.5),
        grid=(B, NSA_KV, nq),
        in_specs=[pl.BlockSpec((tq, gw), lambda b, h, i: (b * nq + i, h)),
                  pl.BlockSpec((1, 1, ncp, NSA_DIM), lambda b, h, i: (b, h, 0, 0)),
                  pl.BlockSpec((1, 1, ncp, NSA_DIM), lambda b, h, i: (b, NSA_KV + h, 0, 0)),
                  pl.BlockSpec((ns, ncp), lambda b, h, i: (0, 0))],
        out_specs=[pl.BlockSpec((tq, gw), lambda b, h, i: (b * nq + i, h)),
                   pl.BlockSpec((1, 1, tq, ns), lambda b, h, i: (b, h, i, 0))],
        out_shape=[jax.ShapeDtypeStruct((T, NSA_HEADS * NSA_DIM), F32),
                   jax.ShapeDtypeStruct((B, NSA_KV, S, ns), BF16)],
        scratch_shapes=[pltpu.VMEM((ns, tq), F32)],
        compiler_params=_cparams(("arbitrary", "arbitrary", "arbitrary")),
        name="nsa_compressed",
    )(q, kvc, kvc, ovt)


def _nsa_sel_kernel(q_ref, k_ref, v_ref, sel_ref, o_ref, *, tq, tk, scale):
    i = pl.program_id(2)
    q0 = i * tq
    rows = NSA_G * tq
    ns = sel_ref.shape[3]
    qq = _stack_heads(q_ref, tq)
    sel = sel_ref[0, 0]

    def step(c, carry):
        m, l, acc = carry
        k0 = pl.multiple_of(c * tk, tk)
        kc = k_ref[pl.ds(k0, tk), :]
        vc = v_ref[pl.ds(k0, tk), :]
        s = lax.dot_general(qq, kc, _NT, preferred_element_type=F32) * scale
        kpos = k0 + lax.broadcasted_iota(jnp.int32, (ns, tk), 1)
        expand = (lax.broadcasted_iota(jnp.int32, (ns, tk), 0) == kpos // SLC_LEN).astype(BF16)
        picked = jnp.dot(sel, expand, preferred_element_type=F32)
        tpos = q0 + lax.broadcasted_iota(jnp.int32, (tq, tk), 0)
        valid = (picked > 0.5) & (k0 + lax.broadcasted_iota(jnp.int32, (tq, tk), 1) <= tpos)
        s3 = jnp.where(valid[None], s.reshape(NSA_G, tq, tk), NEG_INF)
        s = s3.reshape(rows, tk)
        m_new = jnp.maximum(m, jnp.max(s, axis=-1, keepdims=True))
        m_safe = jnp.where(m_new == NEG_INF, 0.0, m_new)
        alpha = jnp.exp(m - m_safe)
        p = jnp.exp(s - m_safe)
        l = alpha * l + jnp.sum(p, axis=-1, keepdims=True)
        acc = alpha * acc + jnp.dot(p.astype(BF16), vc, preferred_element_type=F32)
        return m_new, l, acc

    init = (jnp.full((rows, 1), NEG_INF, F32), jnp.zeros((rows, 1), F32), jnp.zeros((rows, LANES), F32))
    nchunk = (q0 + tq + tk - 1) // tk
    m, l, acc = lax.fori_loop(0, nchunk, step, init)
    o = acc / jnp.maximum(l, 1e-30)
    for g in range(NSA_G):
        o_ref[:, g * LANES:(g + 1) * LANES] = o[g * tq:(g + 1) * tq]


def nsa_selected(q, k, v, sel, B, S, *, k_col0, v_col0, tq=128, tk=256):
    T = B * S
    nq = S // tq
    ns = S // SLC_LEN
    gw = NSA_G * NSA_DIM
    kb, vb = k_col0 // LANES, v_col0 // LANES
    return pl.pallas_call(
        functools.partial(_nsa_sel_kernel, tq=tq, tk=tk, scale=NSA_DIM ** -0.5),
        grid=(B, NSA_KV, nq),
        in_specs=[pl.BlockSpec((tq, gw), lambda b, h, i: (b * nq + i, h)),
                  pl.BlockSpec((S, LANES), lambda b, h, i: (b, kb + h)),
                  pl.BlockSpec((S, LANES), lambda b, h, i: (b, vb + h)),
                  pl.BlockSpec((1, 1, tq, ns), lambda b, h, i: (b, h, i, 0))],
        out_specs=pl.BlockSpec((tq, gw), lambda b, h, i: (b * nq + i, h)),
        out_shape=jax.ShapeDtypeStruct((T, NSA_HEADS * NSA_DIM), F32),
        compiler_params=_cparams(("arbitrary", "arbitrary", "arbitrary")),
        name="nsa_selected",
    )(q, k, v, sel)


def _nsa_win_kernel(q_ref, k_ref, v_ref, o_ref, *, tq, window, scale):
    i = pl.program_id(2)
    q0 = i * tq
    rows = NSA_G * tq
    span = tq + window
    ks = pl.multiple_of(jnp.maximum(q0 - window, 0), tq)
    qq = _stack_heads(q_ref, tq)
    kc = k_ref[pl.ds(ks, span), :]
    vc = v_ref[pl.ds(ks, span), :]
    s = lax.dot_general(qq, kc, _NT, preferred_element_type=F32) * scale
    r = lax.broadcasted_iota(jnp.int32, (rows, span), 0) % tq
    cc = lax.broadcasted_iota(jnp.int32, (rows, span), 1)
    rel = (q0 - ks) + r - cc
    s = jnp.where((rel >= 0) & (rel < window), s, NEG_INF)
    m = jnp.max(s, axis=-1, keepdims=True)
    p = jnp.exp(s - m)
    o = jnp.dot(p.astype(BF16), vc, preferred_element_type=F32) / jnp.sum(p, axis=-1, keepdims=True)
    for g in range(NSA_G):
        o_ref[:, g * LANES:(g + 1) * LANES] = o[g * tq:(g + 1) * tq]


def nsa_window(q, k, v, B, S, *, k_col0, v_col0, tq=128):
    T = B * S
    nq = S // tq
    gw = NSA_G * NSA_DIM
    kb, vb = k_col0 // LANES, v_col0 // LANES
    assert S >= tq + NSA_WINDOW
    return pl.pallas_call(
        functools.partial(_nsa_win_kernel, tq=tq, window=NSA_WINDOW, scale=NSA_DIM ** -0.5),
        grid=(B, NSA_KV, nq),
        in_specs=[pl.BlockSpec((tq, gw), lambda b, h, i: (b * nq + i, h)),
                  pl.BlockSpec((S, LANES), lambda b, h, i: (b, kb + h)),
                  pl.BlockSpec((S, LANES), lambda b, h, i: (b, vb + h))],
        out_specs=pl.BlockSpec((tq, gw), lambda b, h, i: (b * nq + i, h)),
        out_shape=jax.ShapeDtypeStruct((T, NSA_HEADS * NSA_DIM), F32),
        compiler_params=_cparams(("arbitrary", "arbitrary", "arbitrary")),
        name="nsa_window",
    )(q, k, v)


def _nsa_mix_kernel(g_ref, oc_ref, os_ref, ow_ref, o_ref):
    sg = jax.nn.sigmoid(g_ref[...])
    for hq in range(NSA_HEADS):
        sl = slice(hq * LANES, (hq + 1) * LANES)
        acc = sg[:, hq:hq + 1] * oc_ref[:, sl]
        acc = acc + sg[:, NSA_HEADS + hq:NSA_HEADS + hq + 1] * os_ref[:, sl]
        acc = acc + sg[:, 2 * NSA_HEADS + hq:2 * NSA_HEADS + hq + 1] * ow_ref[:, sl]
        o_ref[:, sl] = acc.astype(o_ref.dtype)


def nsa_mix(gates, oc, os_, ow, *, tm=256):
    T, N = oc.shape
    spec = pl.BlockSpec((tm, N), lambda i: (i, 0))
    return pl.pallas_call(
        _nsa_mix_kernel,
        grid=(T // tm,),
        in_specs=[pl.BlockSpec((tm, LANES), lambda i: (i, 0)), spec, spec, spec],
        out_specs=spec,
        out_shape=jax.ShapeDtypeStruct((T, N), BF16),
        compiler_params=_cparams(("arbitrary",)),
        name="nsa_mix",
    )(gates, oc, os_, ow)


def _extract_sorted(s, out_ref, n):
    rowi = lax.broadcasted_iota(jnp.int32, s.shape, 0).astype(F32)

    def body(a, s):
        m = jnp.max(s, axis=0, keepdims=True)
        idx = jnp.min(jnp.where(s == m, rowi, 1e9), axis=0, keepdims=True)
        out_ref[pl.ds(a, 1), :] = m
        return jnp.where(rowi == idx, NEG_INF, s)

    return lax.fori_loop(0, n, body, s)


def _peer_route_kernel(q_ref, sk_ref, s0_ref, s1_ref, e0_ref, e1_ref, th_ref, sv0, sv1, best):
    K = PEER_TOPK
    q = q_ref[...]
    s0 = lax.dot_general(sk_ref[0, 0], q[:, :PEER_HALF], _NT, preferred_element_type=F32)
    s1 = lax.dot_general(sk_ref[0, 1], q[:, PEER_HALF:], _NT, preferred_element_type=F32)
    _extract_sorted(s0, sv0, K)
    _extract_sorted(s1, sv1, K)
    a0 = sv0[...]
    a1 = sv1[...]
    cand = jnp.concatenate([a0[0:1] + a1] + [a0[a:a + 1] + a1[0:8] for a in range(1, 8)]
                           + [a0[8:16] + a1[0:1]], axis=0)
    _extract_sorted(cand, best, K)
    bv = best[...]
    top = bv[0:1]
    z = jnp.sum(jnp.exp(bv - top), axis=0, keepdims=True)
    s0_ref[0] = s0
    s1_ref[0] = s1
    e0_ref[0] = jnp.exp(s0 - a0[0:1])
    e1_ref[0] = jnp.exp(s1 - a1[0:1]) / z
    th_ref[0] = bv[K - 1:K]


def peer_route(q, subkeys, *, tt=256):
    T = q.shape[0]
    H = PEER_HEADS
    big = jax.ShapeDtypeStruct((H, PEER_NKEYS, T), F32)
    bspec = pl.BlockSpec((1, PEER_NKEYS, tt), lambda i, h: (h, 0, i))
    return pl.pallas_call(
        _peer_route_kernel,
        grid=(T // tt, H),
        in_specs=[pl.BlockSpec((tt, 2 * PEER_HALF), lambda i, h: (i, h)),
                  pl.BlockSpec((1, 2, PEER_NKEYS, PEER_HALF), lambda i, h: (h, 0, 0, 0))],
        out_specs=[bspec, bspec, bspec, bspec, pl.BlockSpec((1, 1, tt), lambda i, h: (h, 0, i))],
        out_shape=[big, big, big, big, jax.ShapeDtypeStruct((H, 1, T), F32)],
        scratch_shapes=[pltpu.VMEM((PEER_TOPK, tt), F32)] * 3,
        compiler_params=_cparams(("arbitrary", "arbitrary")),
        name="peer_route",
    )(q, subkeys.astype(BF16))


def _peer_mix_kernel(h_ref, u_ref, vt_ref, s0_ref, s1_ref, e0_ref, e1_ref, th_ref, x_ref, gate_ref,
                     o_ref, acc_ref, *, te):
    e = pl.program_id(1)
    tt = h_ref.shape[0]

    @pl.when(e == 0)
    def _():
        acc_ref[...] = jnp.zeros_like(acc_ref)

    at = lax.dot_general(u_ref[...], h_ref[...], _NT, preferred_element_type=F32)
    nslab = te // PEER_NKEYS
    coefs = []
    for ii in range(nslab):
        irow = e * nslab + ii
        w = jnp.zeros((PEER_NKEYS, tt), F32)
        for hh in range(PEER_HEADS):
            s0r = s0_ref[hh, pl.ds(irow, 1), :]
            e0r = e0_ref[hh, pl.ds(irow, 1), :]
            val = s0r + s1_ref[hh]
            w = w + jnp.where(val >= th_ref[hh], e0r * e1_ref[hh], 0.0)
        a = at[ii * PEER_NKEYS:(ii + 1) * PEER_NKEYS]
        coefs.append((w * jax.nn.gelu(a)).astype(BF16))
    coef = jnp.concatenate(coefs, axis=0)
    acc_ref[...] += jnp.dot(vt_ref[...], coef, preferred_element_type=F32)

    @pl.when(e == pl.num_programs(1) - 1)
    def _():
        o_ref[...] = x_ref[...] + gate_ref[0] * acc_ref[...].T


def peer_mix(h, u, vt, route, x2, gate, S, *, tt=512, te=512):
    T, D = h.shape
    E = u.shape[0]
    H = PEER_HEADS
    per = S // tt
    B = T // S
    s0, s1, e0, e1, th = route
    bspec = pl.BlockSpec((H, PEER_NKEYS, tt), lambda i, e: (0, 0, i))
    return pl.pallas_call(
        functools.partial(_peer_mix_kernel, te=te),
        grid=(T // tt, E // te),
        in_specs=[pl.BlockSpec((tt, D), lambda i, e: (i, 0)),
                  pl.BlockSpec((te, D), lambda i, e: (e, 0)),
                  pl.BlockSpec((D, te), lambda i, e: (0, e)),
                  bspec, bspec, bspec, bspec,
                  pl.BlockSpec((H, 1, tt), lambda i, e: (0, 0, i)),
                  pl.BlockSpec((tt, D), lambda i, e: (i, 0)),
                  pl.BlockSpec((1, 1, D), lambda i, e: (i // per, 0, 0))],
        out_specs=pl.BlockSpec((tt, D), lambda i, e: (i, 0)),
        out_shape=jax.ShapeDtypeStruct((T, D), F32),
        scratch_shapes=[pltpu.VMEM((D, tt), F32)],
        compiler_params=_cparams(("arbitrary", "arbitrary"), vmem_mb=56),
        name="peer_mix",
    )(h, u, vt, s0, s1, e0, e1, th, x2, gate.reshape(B, 1, D))


def _rmsnorm_kernel(x_ref, g_ref, o_ref):
    x = x_ref[...]
    ms = jnp.mean(x * x, axis=-1, keepdims=True)
    o_ref[...] = x * lax.rsqrt(ms + NORM_EPS) * g_ref[...]


def rmsnorm_rows(x2, g, *, tm=512):
    T, D = x2.shape
    return pl.pallas_call(
        _rmsnorm_kernel,
        grid=(T // tm,),
        in_specs=[pl.BlockSpec((tm, D), lambda i: (i, 0)), pl.BlockSpec((1, D), lambda i: (0, 0))],
        out_specs=pl.BlockSpec((tm, D), lambda i: (i, 0)),
        out_shape=jax.ShapeDtypeStruct((T, D), F32),
        compiler_params=_cparams(("arbitrary",)),
        name="final_rmsnorm",
    )(x2, g.reshape(1, D))


def _even_w_in_layout(w):
    D = w.shape[0]
    da = DA_HEADS * DA_DIM
    qa = w[:, 0:2 * da].reshape(D, 2, DA_HEADS, DA_DIM)
    ka = w[:, 2 * da:4 * da].reshape(D, 2, DA_HEADS, DA_DIM)
    o = 4 * da
    va = w[:, o:o + DA_HEADS * DA_VDIM]
    o += DA_HEADS * DA_VDIM
    qs = w[:, o:o + SW_HEADS * SW_DIM]
    o += SW_HEADS * SW_DIM
    ks = w[:, o:o + SW_KV * SW_DIM].reshape(D, SW_KV, 1, SW_DIM)
    o += SW_KV * SW_DIM
    vs = w[:, o:o + SW_KV * SW_DIM].reshape(D, SW_KV, 1, SW_DIM)
    scale = DA_DIM ** -0.5
    assert scale == 0.125 and SW_DIM == DA_DIM
    qa2 = (qa * scale).transpose(0, 2, 1, 3).reshape(D, 2 * da)
    ka2 = ka.transpose(0, 2, 1, 3).reshape(D, 2 * da)
    ks2 = jnp.broadcast_to(ks, (D, SW_KV, 2, SW_DIM)).reshape(D, 2 * SW_KV * SW_DIM)
    vs2 = jnp.broadcast_to(vs, (D, SW_KV, 2, SW_DIM)).reshape(D, 2 * SW_KV * SW_DIM)
    return jnp.concatenate([qa2, ka2, qs * scale, ks2, va, vs2], axis=1).astype(BF16)


def _odd_w_in_layout(w):
    D = w.shape[0]
    kvw = NSA_KV * NSA_DIM
    o = NSA_HEADS * NSA_DIM
    q = w[:, :o]
    kc, vc, ksl, vsl, kw, vw = [w[:, o + n * kvw:o + (n + 1) * kvw] for n in range(6)]
    gates = w[:, o + 6 * kvw:]
    pad = jnp.zeros((D, 3840 - (o + 6 * kvw + gates.shape[1])), w.dtype)
    return jnp.concatenate([q, ksl, kw, vsl, vw, kc, vc, gates, pad], axis=1).astype(BF16)


def _overlap_t(S):
    ncp = S // CMP_STRIDE
    ns = S // SLC_LEN
    cst = np.arange(ncp)[None, :] * CMP_STRIDE
    sst = np.arange(ns)[:, None] * SLC_LEN
    ov = np.clip(np.minimum(cst + CMP_LEN, sst + SLC_LEN) - np.maximum(cst, sst), 0, None)
    return jnp.asarray(ov.astype(np.float32) / CMP_LEN, dtype=BF16)


def _peer_layer(x2, mod, g, wq, subkeys, u_tab, v_tab, B, S):
    D = x2.shape[1]
    shift, scale, gate = mod[:, :D], mod[:, D:2 * D], mod[:, 2 * D:]
    q, h = norm_mod_matmul(x2, g, scale, shift, wq.astype(BF16), S, emit_h=True, out_dtype=BF16)
    route = peer_route(q, subkeys)
    return peer_mix(h, u_tab.astype(BF16), v_tab.astype(BF16).T, route, x2, gate, S)


def _even_layer(x2, mod, g, w_in, w_out, lam_vecs, subln_g, sinks, lam_init, cos64, sin64, B, S):
    D = x2.shape[1]
    shift, scale, gate = mod[:, :D], mod[:, D:2 * D], mod[:, 2 * D:]
    y = norm_mod_matmul(x2, g, scale, shift, _even_w_in_layout(w_in), S)
    qk = rope_cols(y, 0, 3584, cos64, sin64, 64, S)
    v = cast_cols(y, 3584, 1536, BF16, S)
    oa = diff_attention(qk, v, lam_vecs, subln_g, B, S, lam_init, q_col0=0, k_col0=1024, v_col0=0)
    ob = swa_attention(qk, v, sinks, B, S, q_col0=2048, k_col0=3072, v_col0=1024)
    o = jnp.concatenate([oa, ob], axis=1)
    return matmul_residual(o, w_out.astype(BF16), x2, gate, S)


def _odd_layer(x2, mod, g, w_in, w_out, cmp_pos, cmp_w1, cmp_w2, cos128, sin128, cos_c, sin_c, B, S):
    D = x2.shape[1]
    shift, scale, gate = mod[:, :D], mod[:, D:2 * D], mod[:, 2 * D:]
    y = norm_mod_matmul(x2, g, scale, shift, _odd_w_in_layout(w_in), S, tn=768)
    qk = rope_cols(y, 0, 2560, cos128, sin128, 128, S)
    v = cast_cols(y, 2560, 512, BF16, S)
    xc = cast_cols(y, 3072, 512, F32, S)
    gates = cast_cols(y, 3584, 128, F32, S, cw=128)
    kvc = nsa_compress(xc, cmp_pos, cmp_w1, cmp_w2, cos_c, sin_c, B, S)
    oc, sel = nsa_compressed(qk, kvc, _overlap_t(S), B, S)
    os_ = nsa_selected(qk, qk, v, sel, B, S, k_col0=2048, v_col0=0)
    ow = nsa_window(qk, qk, v, B, S, k_col0=2304, v_col0=256)
    o = nsa_mix(gates, oc, os_, ow)
    return matmul_residual(o, w_out.astype(BF16), x2, gate, S)


def kernel(x, c, ada_w, ada_b, norm_g, even_w_in, even_w_out, da_lambda, da_subln, sw_sinks, odd_w_in, odd_w_out, nsa_cmp_pos, nsa_cmp_w1, nsa_cmp_w2, peer_wq, peer_subkeys, peer_u, peer_v, final_g):
    B, S, D = x.shape
    depth = ada_w.shape[0]
    pos = jnp.arange(S)
    cos64, sin64 = _rope_tables(pos, DA_DIM)
    cos128, sin128 = _rope_tables(pos, NSA_DIM)
    cos_c, sin_c = _rope_tables(jnp.arange(S // CMP_STRIDE) * CMP_STRIDE + CMP_LEN - 1, NSA_DIM)
    mods = ada_mod(c, ada_w, ada_b)
    x2 = x.reshape(B * S, D)
    for layer in range(depth):
        j = layer // 2
        if layer % 2 == 0:
            lam_init = 0.8 - 0.6 * math.exp(-0.3 * layer)
            x2 = _even_layer(x2, mods[2 * layer], norm_g[layer, 0], even_w_in[j], even_w_out[j],
                             da_lambda[j], da_subln[j], sw_sinks[j], lam_init, cos64, sin64, B, S)
        else:
            x2 = _odd_layer(x2, mods[2 * layer], norm_g[layer, 0], odd_w_in[j], odd_w_out[j],
                            nsa_cmp_pos[j], nsa_cmp_w1[j], nsa_cmp_w2[j], cos128, sin128, cos_c, sin_c, B, S)
        x2 = _peer_layer(x2, mods[2 * layer + 1], norm_g[layer, 1], peer_wq[layer], peer_subkeys[layer],
                         peer_u[layer], peer_v[layer], B, S)
    return rmsnorm_rows(x2, final_g).reshape(B, S, D)
```

```python
import functools
import math

import jax
import jax.numpy as jnp
import numpy as np
from jax import lax
from jax.experimental import pallas as pl
from jax.experimental.pallas import tpu as pltpu

F32 = jnp.float32
BF16 = jnp.bfloat16
NEG_INF = float("-inf")

D_MODEL = 2048
ROPE_THETA = 10000.0
NORM_EPS = 1e-6

DA_HEADS = 8
DA_DIM = 64
DA_VDIM = 128
SW_HEADS = 16
SW_KV = 4
SW_DIM = 64
SW_WINDOW = 128

NSA_HEADS = 16
NSA_KV = 2
NSA_G = NSA_HEADS // NSA_KV
NSA_DIM = 128
CMP_LEN = 32
CMP_STRIDE = 16
CMP_HIDDEN = 256
SLC_LEN = 64
SLC_TOPN = 16
NSA_WINDOW = 512

PEER_HEADS = 8
PEER_NKEYS = 128
PEER_TOPK = 16
PEER_HALF = 128

LANES = 128
V7X_VMEM_BYTES = 64 * 1024 * 1024

_NT = (((1,), (1,)), ((), ()))


def _cparams(sem, vmem_mb=48):
    assert vmem_mb * 1024 * 1024 < V7X_VMEM_BYTES
    return pltpu.CompilerParams(dimension_semantics=sem, vmem_limit_bytes=vmem_mb * 1024 * 1024)


def _ada_kernel(c_ref, w_ref, b_ref, o_ref):
    c = c_ref[...]
    sc = c * jax.nn.sigmoid(c)
    o_ref[0] = jnp.dot(sc, w_ref[0], preferred_element_type=F32,
                       precision=lax.Precision.HIGHEST) + b_ref[0]


def ada_mod(c, ada_w, ada_b):
    B, D = c.shape
    n = ada_w.shape[0] * ada_w.shape[1]
    w = ada_w.reshape(n, D, 3 * D)
    b = ada_b.reshape(n, 1, 3 * D)
    rows = 8
    cp = jnp.zeros((rows, D), F32).at[:B].set(c)
    tn = 512
    out = pl.pallas_call(
        _ada_kernel,
        grid=(n, 3 * D // tn),
        in_specs=[pl.BlockSpec((rows, D), lambda l, j: (0, 0)),
                  pl.BlockSpec((1, D, tn), lambda l, j: (l, 0, j)),
                  pl.BlockSpec((1, 1, tn), lambda l, j: (l, 0, j))],
        out_specs=pl.BlockSpec((1, rows, tn), lambda l, j: (l, 0, j)),
        out_shape=jax.ShapeDtypeStruct((n, rows, 3 * D), F32),
        compiler_params=_cparams(("arbitrary", "arbitrary")),
        name="ada_mod",
    )(cp, w, b)
    return out[:, :B]


def _nmm_kernel(x_ref, g_ref, sc_ref, sh_ref, w_ref, *rest, emit_h):
    if emit_h:
        y_ref, h_out_ref, hs_ref = rest
    else:
        y_ref, hs_ref = rest

    @pl.when(pl.program_id(1) == 0)
    def _():
        x = x_ref[...]
        ms = jnp.mean(x * x, axis=-1, keepdims=True)
        y = x * lax.rsqrt(ms + NORM_EPS) * g_ref[...]
        h = (y * (1.0 + sc_ref[0]) + sh_ref[0]).astype(BF16)
        hs_ref[...] = h
        if emit_h:
            h_out_ref[...] = h

    y_ref[...] = jnp.dot(hs_ref[...], w_ref[...], preferred_element_type=F32).astype(y_ref.dtype)


def norm_mod_matmul(x2, g, scale, shift, w, S, *, emit_h=False, out_dtype=F32, tm=1024, tn=512):
    T, D = x2.shape
    N = w.shape[1]
    tm = min(tm, S)
    assert S % tm == 0 and N % tn == 0
    per = S // tm
    B = T // S
    out_shape = [jax.ShapeDtypeStruct((T, N), out_dtype)]
    out_specs = [pl.BlockSpec((tm, tn), lambda i, j: (i, j))]
    if emit_h:
        out_shape.append(jax.ShapeDtypeStruct((T, D), BF16))
        out_specs.append(pl.BlockSpec((tm, D), lambda i, j: (i, 0)))
    res = pl.pallas_call(
        functools.partial(_nmm_kernel, emit_h=emit_h),
        grid=(T // tm, N // tn),
        in_specs=[pl.BlockSpec((tm, D), lambda i, j: (i, 0)),
                  pl.BlockSpec((1, D), lambda i, j: (0, 0)),
                  pl.BlockSpec((1, 1, D), lambda i, j: (i // per, 0, 0)),
                  pl.BlockSpec((1, 1, D), lambda i, j: (i // per, 0, 0)),
                  pl.BlockSpec((D, tn), lambda i, j: (0, j))],
        out_specs=out_specs,
        out_shape=out_shape,
        scratch_shapes=[pltpu.VMEM((tm, D), BF16)],
        compiler_params=_cparams(("arbitrary", "arbitrary")),
        name="norm_mod_matmul",
    )(x2, g.reshape(1, D), scale.reshape(B, 1, D), shift.reshape(B, 1, D), w)
    return res if emit_h else res[0]


def _mmres_kernel(a_ref, w_ref, x_ref, gate_ref, o_ref):
    y = jnp.dot(a_ref[...], w_ref[...], preferred_element_type=F32)
    o_ref[...] = x_ref[...] + gate_ref[0] * y


def matmul_residual(a, w, x2, gate, S, *, tm=1024, tn=512):
    T, K = a.shape
    N = w.shape[1]
    tm = min(tm, S)
    per = S // tm
    B = T // S
    return pl.pallas_call(
        _mmres_kernel,
        grid=(T // tm, N // tn),
        in_specs=[pl.BlockSpec((tm, K), lambda i, j: (i, 0)),
                  pl.BlockSpec((K, tn), lambda i, j: (0, j)),
                  pl.BlockSpec((tm, tn), lambda i, j: (i, j)),
                  pl.BlockSpec((1, 1, tn), lambda i, j: (i // per, 0, j))],
        out_specs=pl.BlockSpec((tm, tn), lambda i, j: (i, j)),
        out_shape=jax.ShapeDtypeStruct((T, N), F32),
        compiler_params=_cparams(("arbitrary", "arbitrary")),
        name="matmul_residual",
    )(a, w, x2, gate.reshape(B, 1, N))


def _rope_block(v, cos, sin, hd):
    if hd == 64:
        lane = lax.broadcasted_iota(jnp.int32, v.shape, 1)
        lo = (lane % 64) < 32
        partner = jnp.where(lo, pltpu.roll(v, 96, 1), pltpu.roll(v, 32, 1))
    else:
        partner = pltpu.roll(v, 64, 1)
    return v * cos + partner * sin


def _rope_kernel(y_ref, cos_ref, sin_ref, o_ref, *, hd):
    cos = cos_ref[...]
    sin = sin_ref[...]
    for k in range(o_ref.shape[1] // LANES):
        sl = slice(k * LANES, (k + 1) * LANES)
        o_ref[:, sl] = _rope_block(y_ref[:, sl], cos, sin, hd).astype(o_ref.dtype)


def _cast_kernel(y_ref, o_ref):
    o_ref[...] = y_ref[...].astype(o_ref.dtype)


def rope_cols(y, col0, ncols, cos, sin, hd, S, *, tm=512, cw=512):
    T = y.shape[0]
    tm = min(tm, S)
    per = S // tm
    assert col0 % cw == 0 and ncols % cw == 0
    c0 = col0 // cw
    return pl.pallas_call(
        functools.partial(_rope_kernel, hd=hd),
        grid=(T // tm, ncols // cw),
        in_specs=[pl.BlockSpec((tm, cw), lambda i, j: (i, c0 + j)),
                  pl.BlockSpec((tm, LANES), lambda i, j: (i % per, 0)),
                  pl.BlockSpec((tm, LANES), lambda i, j: (i % per, 0))],
        out_specs=pl.BlockSpec((tm, cw), lambda i, j: (i, j)),
        out_shape=jax.ShapeDtypeStruct((T, ncols), BF16),
        compiler_params=_cparams(("arbitrary", "arbitrary")),
        name="rope_cols",
    )(y, cos, sin)


def cast_cols(y, col0, ncols, dtype, S, *, tm=512, cw=512):
    T = y.shape[0]
    tm = min(tm, S)
    cw = min(cw, ncols)
    assert col0 % cw == 0 and ncols % cw == 0
    c0 = col0 // cw
    return pl.pallas_call(
        _cast_kernel,
        grid=(T // tm, ncols // cw),
        in_specs=[pl.BlockSpec((tm, cw), lambda i, j: (i, c0 + j))],
        out_specs=pl.BlockSpec((tm, cw), lambda i, j: (i, j)),
        out_shape=jax.ShapeDtypeStruct((T, ncols), dtype),
        compiler_params=_cparams(("arbitrary", "arbitrary")),
        name="cast_cols",
    )(y)


def _rope_tables(pos, hd):
    inv = jnp.power(ROPE_THETA, -jnp.arange(0, hd, 2, dtype=F32) / hd)
    ang = pos.astype(F32)[:, None] * inv[None, :]
    cos, sin = jnp.cos(ang), jnp.sin(ang)
    reps = LANES // hd
    cos_l = jnp.tile(jnp.concatenate([cos, cos], axis=1), (1, reps))
    sin_l = jnp.tile(jnp.concatenate([-sin, sin], axis=1), (1, reps))
    return cos_l, sin_l


def _split_halves(q):
    lane = lax.broadcasted_iota(jnp.int32, q.shape, 1)
    zero = jnp.zeros_like(q)
    return jnp.concatenate([jnp.where(lane < 64, q, zero), jnp.where(lane >= 64, q, zero)], axis=0)


def _diff_kernel(q_ref, k_ref, v_ref, lv_ref, sg_ref, o_ref, *, tq, tk, lam_init):
    i = pl.program_id(2)
    q0 = i * tq
    qq = _split_halves(q_ref[...])
    rows = 2 * tq

    def step(c, carry, masked):
        m, l, acc = carry
        k0 = pl.multiple_of(c * tk, tk)
        kc = k_ref[pl.ds(k0, tk), :]
        vc = v_ref[pl.ds(k0, tk), :]
        s = lax.dot_general(qq, kc, _NT, preferred_element_type=F32)
        if masked:
            r = lax.broadcasted_iota(jnp.int32, (rows, tk), 0) % tq
            cc = lax.broadcasted_iota(jnp.int32, (rows, tk), 1)
            s = jnp.where(cc - r <= q0 - k0, s, NEG_INF)
        m_new = jnp.maximum(m, jnp.max(s, axis=-1, keepdims=True))
        alpha = jnp.exp(m - m_new)
        p = jnp.exp(s - m_new)
        l = alpha * l + jnp.sum(p, axis=-1, keepdims=True)
        acc = alpha * acc + jnp.dot(p.astype(BF16), vc, preferred_element_type=F32)
        return m_new, l, acc

    init = (jnp.full((rows, 1), NEG_INF, F32), jnp.zeros((rows, 1), F32), jnp.zeros((rows, LANES), F32))
    nfull = q0 // tk
    carry = lax.fori_loop(0, nfull, functools.partial(step, masked=False), init)
    m, l, acc = step(nfull, carry, True)
    o = acc / l
    lv = lv_ref[...]
    lam = (jnp.exp(jnp.sum(lv[0:1] * lv[1:2], axis=-1, keepdims=True))
           - jnp.exp(jnp.sum(lv[2:3] * lv[3:4], axis=-1, keepdims=True)) + lam_init)
    o = o[:tq] - lam * o[tq:]
    o = o * lax.rsqrt(jnp.mean(o * o, axis=-1, keepdims=True) + NORM_EPS) * sg_ref[...]
    o_ref[...] = (o * (1.0 - lam_init)).astype(o_ref.dtype)


def diff_attention(qk, v, lam_vecs, subln_g, B, S, lam_init, *, q_col0, k_col0, v_col0, tq=256, tk=1024):
    T = B * S
    tq = min(tq, S)
    tk = min(tk, S)
    assert tk % tq == 0 and S % tk == 0
    nq = S // tq
    H = DA_HEADS
    qb, kb, vb = q_col0 // LANES, k_col0 // LANES, v_col0 // LANES
    return pl.pallas_call(
        functools.partial(_diff_kernel, tq=tq, tk=tk, lam_init=lam_init),
        grid=(B, H, nq),
        in_specs=[pl.BlockSpec((tq, LANES), lambda b, h, i: (b * nq + i, qb + h)),
                  pl.BlockSpec((S, LANES), lambda b, h, i: (b, kb + h)),
                  pl.BlockSpec((S, LANES), lambda b, h, i: (b, vb + h)),
                  pl.BlockSpec((4, DA_DIM), lambda b, h, i: (0, 0)),
                  pl.BlockSpec((1, DA_VDIM), lambda b, h, i: (0, 0))],
        out_specs=pl.BlockSpec((tq, LANES), lambda b, h, i: (b * nq + i, h)),
        out_shape=jax.ShapeDtypeStruct((T, H * DA_VDIM), BF16),
        compiler_params=_cparams(("arbitrary", "arbitrary", "arbitrary")),
        name="diff_attention",
    )(qk, qk, v, lam_vecs.astype(F32), subln_g.reshape(1, DA_VDIM).astype(F32))


def _swa_kernel(sink_ref, q_ref, k_ref, v_ref, o_ref, *, tq, window):
    pr = pl.program_id(1)
    i = pl.program_id(2)
    q0 = i * tq
    span = tq + window
    ks = pl.multiple_of(jnp.maximum(q0 - window, 0), window)
    qq = _split_halves(q_ref[...])
    rows = 2 * tq
    kc = k_ref[pl.ds(ks, span), :]
    vc = v_ref[pl.ds(ks, span), :]
    s = lax.dot_general(qq, kc, _NT, preferred_element_type=F32)
    r = lax.broadcasted_iota(jnp.int32, (rows, span), 0) % tq
    cc = lax.broadcasted_iota(jnp.int32, (rows, span), 1)
    rel = (q0 - ks) + r - cc
    s = jnp.where((rel >= 0) & (rel < window), s, NEG_INF)
    rr = lax.broadcasted_iota(jnp.int32, (rows, 1), 0)
    sink = jnp.where(rr < tq, sink_ref[2 * pr], sink_ref[2 * pr + 1])
    m = jnp.maximum(jnp.max(s, axis=-1, keepdims=True), sink)
    p = jnp.exp(s - m)
    den = jnp.sum(p, axis=-1, keepdims=True) + jnp.exp(sink - m)
    o2 = jnp.dot(p.astype(BF16), vc, preferred_element_type=F32) / den
    lane = lax.broadcasted_iota(jnp.int32, (tq, LANES), 1)
    o_ref[...] = jnp.where(lane < 64, o2[:tq], o2[tq:]).astype(o_ref.dtype)


def swa_attention(qk, v, sinks, B, S, *, q_col0, k_col0, v_col0, tq=256):
    T = B * S
    tq = min(tq, S)
    nq = S // tq
    npair = SW_HEADS // 2
    ppk = (SW_HEADS // SW_KV) // 2
    qb, kb, vb = q_col0 // LANES, k_col0 // LANES, v_col0 // LANES
    return pl.pallas_call(
        functools.partial(_swa_kernel, tq=tq, window=SW_WINDOW),
        grid=(B, npair, nq),
        in_specs=[pl.BlockSpec(memory_space=pltpu.SMEM),
                  pl.BlockSpec((tq, LANES), lambda b, p, i: (b * nq + i, qb + p)),
                  pl.BlockSpec((S, LANES), lambda b, p, i: (b, kb + p // ppk)),
                  pl.BlockSpec((S, LANES), lambda b, p, i: (b, vb + p // ppk))],
        out_specs=pl.BlockSpec((tq, LANES), lambda b, p, i: (b * nq + i, p)),
        out_shape=jax.ShapeDtypeStruct((T, SW_HEADS * SW_DIM), BF16),
        compiler_params=_cparams(("arbitrary", "arbitrary", "arbitrary")),
        name="swa_attention",
    )(sinks.astype(F32), qk, qk, v)


def _compress_kernel(x_ref, pos_ref, w1a_ref, w1b_ref, w2_ref, cos_ref, sin_ref, o_ref, acc_a, acc_b):
    tok = pl.program_id(1)
    ntok = pl.num_programs(1)

    @pl.when(tok == 0)
    def _():
        acc_a[...] = jnp.zeros_like(acc_a)
        acc_b[...] = jnp.zeros_like(acc_b)

    for kv in range(2):
        pa = pos_ref[kv, pl.ds(tok, 1), :]
        pb = pos_ref[kv, pl.ds(CMP_STRIDE + tok, 1), :]
        for h in range(NSA_KV):
            idx = kv * NSA_KV + h
            xs = x_ref[0, :, idx * LANES:(idx + 1) * LANES]
            acc_a[idx] += jnp.dot((xs + pa).astype(BF16), w1a_ref[kv], preferred_element_type=F32)
            acc_b[idx] += jnp.dot((xs + pb).astype(BF16), w1b_ref[kv], preferred_element_type=F32)

    @pl.when(tok == ntok - 1)
    def _():
        ncp = acc_a.shape[1]
        for kv in range(2):
            for h in range(NSA_KV):
                idx = kv * NSA_KV + h
                hid = acc_a[idx] + pltpu.roll(acc_b[idx], ncp - 1, 0)
                out = jnp.dot(jax.nn.gelu(hid).astype(BF16), w2_ref[kv], preferred_element_type=F32)
                if kv == 0:
                    out = _rope_block(out, cos_ref[...], sin_ref[...], NSA_DIM)
                o_ref[0, idx] = out.astype(o_ref.dtype)


def nsa_compress(xc, cmp_pos, cmp_w1, cmp_w2, cos_c, sin_c, B, S):
    ncp = S // CMP_STRIDE
    width = 2 * NSA_KV * NSA_DIM
    x3 = xc.reshape(B, ncp, CMP_STRIDE * width)
    half = CMP_STRIDE * NSA_DIM
    w1 = cmp_w1.astype(BF16)
    return pl.pallas_call(
        _compress_kernel,
        grid=(B, CMP_STRIDE),
        in_specs=[pl.BlockSpec((1, ncp, width), lambda b, t: (b, 0, t)),
                  pl.BlockSpec((2, CMP_LEN, NSA_DIM), lambda b, t: (0, 0, 0)),
                  pl.BlockSpec((2, NSA_DIM, CMP_HIDDEN), lambda b, t: (0, t, 0)),
                  pl.BlockSpec((2, NSA_DIM, CMP_HIDDEN), lambda b, t: (0, CMP_STRIDE + t, 0)),
                  pl.BlockSpec((2, CMP_HIDDEN, NSA_DIM), lambda b, t: (0, 0, 0)),
                  pl.BlockSpec((ncp, LANES), lambda b, t: (0, 0)),
                  pl.BlockSpec((ncp, LANES), lambda b, t: (0, 0))],
        out_specs=pl.BlockSpec((1, 2 * NSA_KV, ncp, NSA_DIM), lambda b, t: (b, 0, 0, 0)),
        out_shape=jax.ShapeDtypeStruct((B, 2 * NSA_KV, ncp, NSA_DIM), BF16),
        scratch_shapes=[pltpu.VMEM((2 * NSA_KV, ncp, CMP_HIDDEN), F32),
                        pltpu.VMEM((2 * NSA_KV, ncp, CMP_HIDDEN), F32)],
        compiler_params=_cparams(("arbitrary", "arbitrary")),
        name="nsa_compress",
    )(x3, cmp_pos.astype(F32), w1, w1, cmp_w2.astype(BF16), cos_c, sin_c)


def _stack_heads(q_ref, tq):
    return jnp.concatenate([q_ref[:, g * LANES:(g + 1) * LANES] for g in range(NSA_G)], axis=0)


def _nsa_cmp_kernel(q_ref, kc_ref, vc_ref, ovt_ref, o_ref, sel_ref, imp_scr, *, tq, scale):
    i = pl.program_id(2)
    q0 = i * tq
    rows = NSA_G * tq
    ncp = kc_ref.shape[2]
    ns = ovt_ref.shape[0]
    qq = _stack_heads(q_ref, tq)
    s = lax.dot_general(qq, kc_ref[0, 0], _NT, preferred_element_type=F32) * scale
    tpos = q0 + lax.broadcasted_iota(jnp.int32, (rows, ncp), 0) % tq
    cend = lax.broadcasted_iota(jnp.int32, (rows, ncp), 1) * CMP_STRIDE + (CMP_LEN - 1)
    s = jnp.where(cend <= tpos, s, NEG_INF)
    m = jnp.max(s, axis=-1, keepdims=True)
    m = jnp.where(m == NEG_INF, 0.0, m)
    p = jnp.exp(s - m)
    p = p / jnp.maximum(jnp.sum(p, axis=-1, keepdims=True), 1e-30)
    o = jnp.dot(p.astype(BF16), vc_ref[0, 0], preferred_element_type=F32)
    for g in range(NSA_G):
        o_ref[:, g * LANES:(g + 1) * LANES] = o[g * tq:(g + 1) * tq]
    psum = p[0:tq]
    for g in range(1, NSA_G):
        psum = psum + p[g * tq:(g + 1) * tq]
    p_hi = psum.astype(BF16)
    p_lo = (psum - p_hi.astype(F32)).astype(BF16)
    ovt = ovt_ref[...]
    imp = (lax.dot_general(ovt, p_hi, _NT, preferred_element_type=F32)
           + lax.dot_general(ovt, p_lo, _NT, preferred_element_type=F32))
    blk = lax.broadcasted_iota(jnp.int32, (ns, tq), 0)
    cur = (q0 + lax.broadcasted_iota(jnp.int32, (ns, tq), 1)) // SLC_LEN
    future = blk > cur
    forced = (blk == 0) | (blk == cur) | (blk == cur - 1)
    imp = jnp.where(forced, jnp.inf, imp)
    imp = jnp.where(future, NEG_INF, imp)
    imp_scr[...] = imp

    def count(sp, cnt):
        row = imp_scr[pl.ds(sp, 1), :]
        beats = (row > imp) | ((row == imp) & (sp < blk))
        return cnt + beats.astype(F32)

    cnt = lax.fori_loop(0, ns, count, jnp.zeros((ns, tq), F32))
    sel_t = jnp.where((cnt < float(SLC_TOPN)) & jnp.logical_not(future), 1.0, 0.0)
    sel_ref[0, 0] = sel_t.T.astype(sel_ref.dtype)


def nsa_compressed(q, kvc, ovt, B, S, *, tq=128):
    T = B * S
    nq = S // tq
    ncp = S // CMP_STRIDE
    ns = S // SLC_LEN
    gw = NSA_G * NSA_DIM
    return pl.pallas_call(
        functools.partial(_nsa_cmp_kernel, tq=tq, scale=NSA_DIM ** -0.5),
        grid=(B, NSA_KV, nq),
        in_specs=[pl.BlockSpec((tq, gw), lambda b, h, i: (b * nq + i, h)),
                  pl.BlockSpec((1, 1, ncp, NSA_DIM), lambda b, h, i: (b, h, 0, 0)),
                  pl.BlockSpec((1, 1, ncp, NSA_DIM), lambda b, h, i: (b, NSA_KV + h, 0, 0)),
                  pl.BlockSpec((ns, ncp), lambda b, h, i: (0, 0))],
        out_specs=[pl.BlockSpec((tq, gw), lambda b, h, i: (b * nq + i, h)),
                   pl.BlockSpec((1, 1, tq, ns), lambda b, h, i: (b, h, i, 0))],
        out_shape=[jax.ShapeDtypeStruct((T, NSA_HEADS * NSA_DIM), F32),
                   jax.ShapeDtypeStruct((B, NSA_KV, S, ns), BF16)],
        scratch_shapes=[pltpu.VMEM((ns, tq), F32)],
        compiler_params=_cparams(("arbitrary", "arbitrary", "arbitrary")),
        name="nsa_compressed",
    )(q, kvc, kvc, ovt)


def _nsa_sel_kernel(q_ref, k_ref, v_ref, sel_ref, o_ref, *, tq, tk, scale):
    i = pl.program_id(2)
    q0 = i * tq
    rows = NSA_G * tq
    ns = sel_ref.shape[3]
    qq = _stack_heads(q_ref, tq)
    sel = sel_ref[0, 0]

    blk_off = (lax.broadcasted_iota(jnp.int32, (ns, tk), 0)
               - lax.broadcasted_iota(jnp.int32, (ns, tk), 1) // SLC_LEN)
    key_off = lax.broadcasted_iota(jnp.int32, (tq, tk), 1) - lax.broadcasted_iota(jnp.int32, (tq, tk), 0)

    def step(c, carry):
        m, l, acc = carry
        k0 = pl.multiple_of(c * tk, tk)
        kc = k_ref[pl.ds(k0, tk), :]
        vc = v_ref[pl.ds(k0, tk), :]
        s = lax.dot_general(qq, kc, _NT, preferred_element_type=F32) * scale
        expand = jnp.where(blk_off == c * (tk // SLC_LEN), 1.0, 0.0).astype(BF16)
        picked = jnp.dot(sel, expand, preferred_element_type=F32)
        valid = (picked > 0.5) & (key_off <= q0 - k0)
        s3 = jnp.where(valid[None], s.reshape(NSA_G, tq, tk), NEG_INF)
        s = s3.reshape(rows, tk)
        m_new = jnp.maximum(m, jnp.max(s, axis=-1, keepdims=True))
        m_safe = jnp.where(m_new == NEG_INF, 0.0, m_new)
        alpha = jnp.exp(m - m_safe)
        p = jnp.exp(s - m_safe)
        l = alpha * l + jnp.sum(p, axis=-1, keepdims=True)
        acc = alpha * acc + jnp.dot(p.astype(BF16), vc, preferred_element_type=F32)
        return m_new, l, acc

    init = (jnp.full((rows, 1), NEG_INF, F32), jnp.zeros((rows, 1), F32), jnp.zeros((rows, LANES), F32))
    nchunk = (q0 + tq + tk - 1) // tk
    m, l, acc = lax.fori_loop(0, nchunk, step, init)
    o = acc / jnp.maximum(l, 1e-30)
    for g in range(NSA_G):
        o_ref[:, g * LANES:(g + 1) * LANES] = o[g * tq:(g + 1) * tq]


def nsa_selected(q, k, v, sel, B, S, *, k_col0, v_col0, tq=128, tk=1024):
    T = B * S
    tk = min(tk, S)
    assert S % tk == 0 and tk % SLC_LEN == 0
    nq = S // tq
    ns = S // SLC_LEN
    gw = NSA_G * NSA_DIM
    kb, vb = k_col0 // LANES, v_col0 // LANES
    return pl.pallas_call(
        functools.partial(_nsa_sel_kernel, tq=tq, tk=tk, scale=NSA_DIM ** -0.5),
        grid=(B, NSA_KV, nq),
        in_specs=[pl.BlockSpec((tq, gw), lambda b, h, i: (b * nq + i, h)),
                  pl.BlockSpec((S, LANES), lambda b, h, i: (b, kb + h)),
                  pl.BlockSpec((S, LANES), lambda b, h, i: (b, vb + h)),
                  pl.BlockSpec((1, 1, tq, ns), lambda b, h, i: (b, h, i, 0))],
        out_specs=pl.BlockSpec((tq, gw), lambda b, h, i: (b * nq + i, h)),
        out_shape=jax.ShapeDtypeStruct((T, NSA_HEADS * NSA_DIM), F32),
        compiler_params=_cparams(("arbitrary", "arbitrary", "arbitrary")),
        name="nsa_selected",
    )(q, k, v, sel)


def _nsa_win_kernel(q_ref, k_ref, v_ref, o_ref, *, tq, window, scale):
    i = pl.program_id(2)
    q0 = i * tq
    rows = NSA_G * tq
    span = tq + window
    ks = pl.multiple_of(jnp.maximum(q0 - window, 0), tq)
    qq = _stack_heads(q_ref, tq)
    kc = k_ref[pl.ds(ks, span), :]
    vc = v_ref[pl.ds(ks, span), :]
    s = lax.dot_general(qq, kc, _NT, preferred_element_type=F32) * scale
    r = lax.broadcasted_iota(jnp.int32, (rows, span), 0) % tq
    cc = lax.broadcasted_iota(jnp.int32, (rows, span), 1)
    rel = (q0 - ks) + r - cc
    s = jnp.where((rel >= 0) & (rel < window), s, NEG_INF)
    m = jnp.max(s, axis=-1, keepdims=True)
    p = jnp.exp(s - m)
    o = jnp.dot(p.astype(BF16), vc, preferred_element_type=F32) / jnp.sum(p, axis=-1, keepdims=True)
    for g in range(NSA_G):
        o_ref[:, g * LANES:(g + 1) * LANES] = o[g * tq:(g + 1) * tq]


def nsa_window(q, k, v, B, S, *, k_col0, v_col0, tq=128):
    T = B * S
    nq = S // tq
    gw = NSA_G * NSA_DIM
    kb, vb = k_col0 // LANES, v_col0 // LANES
    assert S >= tq + NSA_WINDOW
    return pl.pallas_call(
        functools.partial(_nsa_win_kernel, tq=tq, window=NSA_WINDOW, scale=NSA_DIM ** -0.5),
        grid=(B, NSA_KV, nq),
        in_specs=[pl.BlockSpec((tq, gw), lambda b, h, i: (b * nq + i, h)),
                  pl.BlockSpec((S, LANES), lambda b, h, i: (b, kb + h)),
                  pl.BlockSpec((S, LANES), lambda b, h, i: (b, vb + h))],
        out_specs=pl.BlockSpec((tq, gw), lambda b, h, i: (b * nq + i, h)),
        out_shape=jax.ShapeDtypeStruct((T, NSA_HEADS * NSA_DIM), F32),
        compiler_params=_cparams(("arbitrary", "arbitrary", "arbitrary")),
        name="nsa_window",
    )(q, k, v)


def _nsa_mix_kernel(g_ref, oc_ref, os_ref, ow_ref, o_ref):
    sg = jax.nn.sigmoid(g_ref[...])
    for hq in range(NSA_HEADS):
        sl = slice(hq * LANES, (hq + 1) * LANES)
        acc = sg[:, hq:hq + 1] * oc_ref[:, sl]
        acc = acc + sg[:, NSA_HEADS + hq:NSA_HEADS + hq + 1] * os_ref[:, sl]
        acc = acc + sg[:, 2 * NSA_HEADS + hq:2 * NSA_HEADS + hq + 1] * ow_ref[:, sl]
        o_ref[:, sl] = acc.astype(o_ref.dtype)


def nsa_mix(gates, oc, os_, ow, *, tm=256):
    T, N = oc.shape
    spec = pl.BlockSpec((tm, N), lambda i: (i, 0))
    return pl.pallas_call(
        _nsa_mix_kernel,
        grid=(T // tm,),
        in_specs=[pl.BlockSpec((tm, LANES), lambda i: (i, 0)), spec, spec, spec],
        out_specs=spec,
        out_shape=jax.ShapeDtypeStruct((T, N), BF16),
        compiler_params=_cparams(("arbitrary",)),
        name="nsa_mix",
    )(gates, oc, os_, ow)


NO_RANK = 127.0


def _extract_sorted(s, out_ref, n):
    rowi = lax.broadcasted_iota(jnp.int32, s.shape, 0).astype(F32)

    def body(a, carry):
        s, rank = carry
        m = jnp.max(s, axis=0, keepdims=True)
        idx = jnp.min(jnp.where(s == m, rowi, 1e9), axis=0, keepdims=True)
        out_ref[pl.ds(a, 1), :] = m
        hit = rowi == idx
        return jnp.where(hit, NEG_INF, s), jnp.where(hit, a.astype(F32), rank)

    return lax.fori_loop(0, n, body, (s, jnp.full(s.shape, NO_RANK, F32)))


def _peer_route_kernel(q_ref, sk_ref, n0_ref, e0_ref, r1_ref, e1_ref, sv01, best):
    K = PEER_TOPK
    q = q_ref[...]
    tt = q.shape[0]
    s0 = lax.dot_general(sk_ref[0, 0], q[:, :PEER_HALF], _NT, preferred_element_type=F32)
    s1 = lax.dot_general(sk_ref[0, 1], q[:, PEER_HALF:], _NT, preferred_element_type=F32)
    _, rank01 = _extract_sorted(jnp.concatenate([s0, s1], axis=1), sv01, K)
    a0 = sv01[:, :tt]
    a1 = sv01[:, tt:]
    rank0 = rank01[:, :tt]
    cand = jnp.concatenate([a0[0:1] + a1] + [a0[a:a + 1] + a1[0:8] for a in range(1, 8)]
                           + [a0[8:16] + a1[0:1]], axis=0)
    left, _ = _extract_sorted(cand, best, K)
    taken = jnp.where(left == NEG_INF, 1.0, 0.0)
    bv = best[...]
    z = jnp.sum(jnp.exp(bv - bv[0:1]), axis=0, keepdims=True)
    n0 = jnp.where(rank0 == 0.0, jnp.sum(taken[0:16], axis=0, keepdims=True), 0.0)
    for a in range(1, 8):
        cnt = jnp.sum(taken[8 + 8 * a:16 + 8 * a], axis=0, keepdims=True)
        n0 = n0 + jnp.where(rank0 == float(a), cnt, 0.0)
    tail = taken[72:80]
    for a in range(8, 16):
        n0 = n0 + jnp.where(rank0 == float(a), tail[a - 8:a - 7], 0.0)
    n0_ref[0] = n0
    e0_ref[0] = jnp.exp(s0 - a0[0:1])
    r1_ref[0] = rank01[:, tt:]
    e1_ref[0] = jnp.exp(s1 - a1[0:1]) / z


def peer_route(q, subkeys, *, tt=256):
    T = q.shape[0]
    H = PEER_HEADS
    f32 = jax.ShapeDtypeStruct((H, PEER_NKEYS, T), F32)
    bspec = pl.BlockSpec((1, PEER_NKEYS, tt), lambda i, h: (h, 0, i))
    return pl.pallas_call(
        _peer_route_kernel,
        grid=(T // tt, H),
        in_specs=[pl.BlockSpec((tt, 2 * PEER_HALF), lambda i, h: (i, h)),
                  pl.BlockSpec((1, 2, PEER_NKEYS, PEER_HALF), lambda i, h: (h, 0, 0, 0))],
        out_specs=[bspec, bspec, bspec, bspec],
        out_shape=[f32, f32, f32, f32],
        scratch_shapes=[pltpu.VMEM((PEER_TOPK, 2 * tt), F32), pltpu.VMEM((PEER_TOPK, tt), F32)],
        compiler_params=_cparams(("arbitrary", "arbitrary")),
        name="peer_route",
    )(q, subkeys.astype(BF16))


def _peer_mix_kernel(ht_ref, u_ref, vt_ref, n0_ref, e0_ref, r1_ref, e1_ref, x_ref,
                     gate_ref, o_ref, acc_ref, at0, at1, cf0, cf1, *, te):
    g = pl.program_id(1)
    tt = ht_ref.shape[1]
    nslab = te // PEER_NKEYS

    @pl.when(g == 0)
    def _():
        acc_ref[...] = jnp.zeros_like(acc_ref)

    def build(at_ref, cf_ref, row0):
        for ii in range(nslab):
            rows = slice(ii * PEER_NKEYS, (ii + 1) * PEER_NKEYS)
            n0rows = [n0_ref[hh, pl.ds(row0 + ii, 1), :] for hh in range(PEER_HEADS)]
            e0rows = [e0_ref[hh, pl.ds(row0 + ii, 1), :] for hh in range(PEER_HEADS)]
            for ts in range(tt // LANES):
                lanes = slice(ts * LANES, (ts + 1) * LANES)
                w = None
                for hh in range(PEER_HEADS):
                    term = jnp.where(r1_ref[hh, :, lanes] < n0rows[hh][:, lanes],
                                     e0rows[hh][:, lanes] * e1_ref[hh, :, lanes], 0.0)
                    w = term if w is None else w + term
                cf_ref[rows, lanes] = (w * jax.nn.gelu(at_ref[rows, lanes])).astype(BF16)

    for t, (at_ref, cf_ref) in enumerate(((at0, cf0), (at1, cf1))):
        er = slice(t * te, (t + 1) * te)
        at_ref[...] = jnp.dot(u_ref[er, :], ht_ref[...], preferred_element_type=F32)
        build(at_ref, cf_ref, (2 * g + t) * nslab)
        acc_ref[...] += jnp.dot(vt_ref[:, er], cf_ref[...], preferred_element_type=F32)

    @pl.when(g == pl.num_programs(1) - 1)
    def _():
        o_ref[...] = x_ref[...] + gate_ref[0] * acc_ref[...].T


def peer_mix(ht, u, vt, route, x2, gate, S, *, tt=512, te=512):
    D, T = ht.shape
    E = u.shape[0]
    H = PEER_HEADS
    per = S // tt
    B = T // S
    n0, e0, r1, e1 = route
    once = pl.Buffered(1)
    bspec = pl.BlockSpec((H, PEER_NKEYS, tt), lambda i, g: (0, 0, i), pipeline_mode=once)
    return pl.pallas_call(
        functools.partial(_peer_mix_kernel, te=te),
        grid=(T // tt, E // (2 * te)),
        in_specs=[pl.BlockSpec((D, tt), lambda i, g: (0, i)),
                  pl.BlockSpec((2 * te, D), lambda i, g: (g, 0)),
                  pl.BlockSpec((D, 2 * te), lambda i, g: (0, g)),
                  bspec, bspec, bspec, bspec,
                  pl.BlockSpec((tt, D), lambda i, g: (i, 0), pipeline_mode=once),
                  pl.BlockSpec((1, 1, D), lambda i, g: (i // per, 0, 0))],
        out_specs=pl.BlockSpec((tt, D), lambda i, g: (i, 0)),
        out_shape=jax.ShapeDtypeStruct((T, D), F32),
        scratch_shapes=[pltpu.VMEM((D, tt), F32), pltpu.VMEM((te, tt), F32), pltpu.VMEM((te, tt), F32),
                        pltpu.VMEM((te, tt), BF16), pltpu.VMEM((te, tt), BF16)],
        compiler_params=_cparams(("arbitrary", "arbitrary"), vmem_mb=56),
        name="peer_mix",
    )(ht, u, vt, n0, e0, r1, e1, x2, gate.reshape(B, 1, D))


def _rmsnorm_kernel(x_ref, g_ref, o_ref):
    x = x_ref[...]
    ms = jnp.mean(x * x, axis=-1, keepdims=True)
    o_ref[...] = x * lax.rsqrt(ms + NORM_EPS) * g_ref[...]


def rmsnorm_rows(x2, g, *, tm=512):
    T, D = x2.shape
    return pl.pallas_call(
        _rmsnorm_kernel,
        grid=(T // tm,),
        in_specs=[pl.BlockSpec((tm, D), lambda i: (i, 0)), pl.BlockSpec((1, D), lambda i: (0, 0))],
        out_specs=pl.BlockSpec((tm, D), lambda i: (i, 0)),
        out_shape=jax.ShapeDtypeStruct((T, D), F32),
        compiler_params=_cparams(("arbitrary",)),
        name="final_rmsnorm",
    )(x2, g.reshape(1, D))


def _even_w_in_layout(w):
    D = w.shape[0]
    da = DA_HEADS * DA_DIM
    qa = w[:, 0:2 * da].reshape(D, 2, DA_HEADS, DA_DIM)
    ka = w[:, 2 * da:4 * da].reshape(D, 2, DA_HEADS, DA_DIM)
    o = 4 * da
    va = w[:, o:o + DA_HEADS * DA_VDIM]
    o += DA_HEADS * DA_VDIM
    qs = w[:, o:o + SW_HEADS * SW_DIM]
    o += SW_HEADS * SW_DIM
    ks = w[:, o:o + SW_KV * SW_DIM].reshape(D, SW_KV, 1, SW_DIM)
    o += SW_KV * SW_DIM
    vs = w[:, o:o + SW_KV * SW_DIM].reshape(D, SW_KV, 1, SW_DIM)
    scale = DA_DIM ** -0.5
    assert scale == 0.125 and SW_DIM == DA_DIM
    qa2 = (qa * scale).transpose(0, 2, 1, 3).reshape(D, 2 * da)
    ka2 = ka.transpose(0, 2, 1, 3).reshape(D, 2 * da)
    ks2 = jnp.broadcast_to(ks, (D, SW_KV, 2, SW_DIM)).reshape(D, 2 * SW_KV * SW_DIM)
    vs2 = jnp.broadcast_to(vs, (D, SW_KV, 2, SW_DIM)).reshape(D, 2 * SW_KV * SW_DIM)
    return jnp.concatenate([qa2, ka2, qs * scale, ks2, va, vs2], axis=1).astype(BF16)


def _odd_w_in_layout(w):
    D = w.shape[0]
    kvw = NSA_KV * NSA_DIM
    o = NSA_HEADS * NSA_DIM
    q = w[:, :o]
    kc, vc, ksl, vsl, kw, vw = [w[:, o + n * kvw:o + (n + 1) * kvw] for n in range(6)]
    gates = w[:, o + 6 * kvw:]
    pad = jnp.zeros((D, 3840 - (o + 6 * kvw + gates.shape[1])), w.dtype)
    return jnp.concatenate([q, ksl, kw, vsl, vw, kc, vc, gates, pad], axis=1).astype(BF16)


def _overlap_t(S):
    ncp = S // CMP_STRIDE
    ns = S // SLC_LEN
    cst = np.arange(ncp)[None, :] * CMP_STRIDE
    sst = np.arange(ns)[:, None] * SLC_LEN
    ov = np.clip(np.minimum(cst + CMP_LEN, sst + SLC_LEN) - np.maximum(cst, sst), 0, None)
    return jnp.asarray(ov.astype(np.float32) / CMP_LEN, dtype=BF16)


def _peer_layer(x2, mod, g, wq, subkeys, u_tab, v_tab, B, S):
    D = x2.shape[1]
    shift, scale, gate = mod[:, :D], mod[:, D:2 * D], mod[:, 2 * D:]
    q, h = norm_mod_matmul(x2, g, scale, shift, wq.astype(BF16), S, emit_h=True, out_dtype=BF16)
    route = peer_route(q, subkeys)
    return peer_mix(h.T, u_tab.astype(BF16), v_tab.astype(BF16).T, route, x2, gate, S)


def _even_layer(x2, mod, g, w_in, w_out, lam_vecs, subln_g, sinks, lam_init, cos64, sin64, B, S):
    D = x2.shape[1]
    shift, scale, gate = mod[:, :D], mod[:, D:2 * D], mod[:, 2 * D:]
    y = norm_mod_matmul(x2, g, scale, shift, _even_w_in_layout(w_in), S)
    qk = rope_cols(y, 0, 3584, cos64, sin64, 64, S)
    v = cast_cols(y, 3584, 1536, BF16, S)
    oa = diff_attention(qk, v, lam_vecs, subln_g, B, S, lam_init, q_col0=0, k_col0=1024, v_col0=0)
    ob = swa_attention(qk, v, sinks, B, S, q_col0=2048, k_col0=3072, v_col0=1024)
    o = jnp.concatenate([oa, ob], axis=1)
    return matmul_residual(o, w_out.astype(BF16), x2, gate, S)


def _odd_layer(x2, mod, g, w_in, w_out, cmp_pos, cmp_w1, cmp_w2, cos128, sin128, cos_c, sin_c, B, S):
    D = x2.shape[1]
    shift, scale, gate = mod[:, :D], mod[:, D:2 * D], mod[:, 2 * D:]
    y = norm_mod_matmul(x2, g, scale, shift, _odd_w_in_layout(w_in), S, tn=768)
    qk = rope_cols(y, 0, 2560, cos128, sin128, 128, S)
    v = cast_cols(y, 2560, 512, BF16, S)
    xc = cast_cols(y, 3072, 512, F32, S)
    gates = cast_cols(y, 3584, 128, F32, S, cw=128)
    kvc = nsa_compress(xc, cmp_pos, cmp_w1, cmp_w2, cos_c, sin_c, B, S)
    oc, sel = nsa_compressed(qk, kvc, _overlap_t(S), B, S)
    os_ = nsa_selected(qk, qk, v, sel, B, S, k_col0=2048, v_col0=0)
    ow = nsa_window(qk, qk, v, B, S, k_col0=2304, v_col0=256)
    o = nsa_mix(gates, oc, os_, ow)
    return matmul_residual(o, w_out.astype(BF16), x2, gate, S)


def kernel(x, c, ada_w, ada_b, norm_g, even_w_in, even_w_out, da_lambda, da_subln, sw_sinks, odd_w_in, odd_w_out, nsa_cmp_pos, nsa_cmp_w1, nsa_cmp_w2, peer_wq, peer_subkeys, peer_u, peer_v, final_g):
    B, S, D = x.shape
    depth = ada_w.shape[0]
    pos = jnp.arange(S)
    cos64, sin64 = _rope_tables(pos, DA_DIM)
    cos128, sin128 = _rope_tables(pos, NSA_DIM)
    cos_c, sin_c = _rope_tables(jnp.arange(S // CMP_STRIDE) * CMP_STRIDE + CMP_LEN - 1, NSA_DIM)
    mods = ada_mod(c, ada_w, ada_b)
    x2 = x.reshape(B * S, D)
    for layer in range(depth):
        j = layer // 2
        if layer % 2 == 0:
            lam_init = 0.8 - 0.6 * math.exp(-0.3 * layer)
            x2 = _even_layer(x2, mods[2 * layer], norm_g[layer, 0], even_w_in[j], even_w_out[j],
                             da_lambda[j], da_subln[j], sw_sinks[j], lam_init, cos64, sin64, B, S)
        else:
            x2 = _odd_layer(x2, mods[2 * layer], norm_g[layer, 0], odd_w_in[j], odd_w_out[j],
                            nsa_cmp_pos[j], nsa_cmp_w1[j], nsa_cmp_w2[j], cos128, sin128, cos_c, sin_c, B, S)
        x2 = _peer_layer(x2, mods[2 * layer + 1], norm_g[layer, 1], peer_wq[layer], peer_subkeys[layer],
                         peer_u[layer], peer_v[layer], B, S)
    return rmsnorm_rows(x2, final_g).reshape(B, S, D)
```

```python
import functools
import math

import jax
import jax.numpy as jnp
import numpy as np
from jax import lax
from jax.experimental import pallas as pl
from jax.experimental.pallas import tpu as pltpu

F32 = jnp.float32
BF16 = jnp.bfloat16
NEG_INF = float("-inf")

D_MODEL = 2048
ROPE_THETA = 10000.0
NORM_EPS = 1e-6

DA_HEADS = 8
DA_DIM = 64
DA_VDIM = 128
SW_HEADS = 16
SW_KV = 4
SW_DIM = 64
SW_WINDOW = 128

NSA_HEADS = 16
NSA_KV = 2
NSA_G = NSA_HEADS // NSA_KV
NSA_DIM = 128
CMP_LEN = 32
CMP_STRIDE = 16
CMP_HIDDEN = 256
SLC_LEN = 64
SLC_TOPN = 16
NSA_WINDOW = 512

PEER_HEADS = 8
PEER_NKEYS = 128
PEER_TOPK = 16
PEER_HALF = 128

LANES = 128
V7X_VMEM_BYTES = 64 * 1024 * 1024

_NT = (((1,), (1,)), ((), ()))


def _cparams(sem, vmem_mb=48):
    assert vmem_mb * 1024 * 1024 < V7X_VMEM_BYTES
    return pltpu.CompilerParams(dimension_semantics=sem, vmem_limit_bytes=vmem_mb * 1024 * 1024)


def _ada_kernel(c_ref, w_ref, b_ref, o_ref):
    c = c_ref[...]
    sc = c * jax.nn.sigmoid(c)
    o_ref[0] = jnp.dot(sc, w_ref[0], preferred_element_type=F32,
                       precision=lax.Precision.HIGHEST) + b_ref[0]


def ada_mod(c, ada_w, ada_b):
    B, D = c.shape
    n = ada_w.shape[0] * ada_w.shape[1]
    w = ada_w.reshape(n, D, 3 * D)
    b = ada_b.reshape(n, 1, 3 * D)
    rows = 8
    cp = jnp.zeros((rows, D), F32).at[:B].set(c)
    tn = 512
    out = pl.pallas_call(
        _ada_kernel,
        grid=(n, 3 * D // tn),
        in_specs=[pl.BlockSpec((rows, D), lambda l, j: (0, 0)),
                  pl.BlockSpec((1, D, tn), lambda l, j: (l, 0, j)),
                  pl.BlockSpec((1, 1, tn), lambda l, j: (l, 0, j))],
        out_specs=pl.BlockSpec((1, rows, tn), lambda l, j: (l, 0, j)),
        out_shape=jax.ShapeDtypeStruct((n, rows, 3 * D), F32),
        compiler_params=_cparams(("arbitrary", "arbitrary")),
        name="ada_mod",
    )(cp, w, b)
    return out[:, :B]


def _nmm_kernel(x_ref, g_ref, sc_ref, sh_ref, w_ref, *rest, emit_h):
    if emit_h:
        y_ref, h_out_ref, hs_ref = rest
    else:
        y_ref, hs_ref = rest

    @pl.when(pl.program_id(1) == 0)
    def _():
        x = x_ref[...]
        ms = jnp.mean(x * x, axis=-1, keepdims=True)
        y = x * lax.rsqrt(ms + NORM_EPS) * g_ref[...]
        h = (y * (1.0 + sc_ref[0]) + sh_ref[0]).astype(BF16)
        hs_ref[...] = h
        if emit_h:
            h_out_ref[...] = h

    y_ref[...] = jnp.dot(hs_ref[...], w_ref[...], preferred_element_type=F32).astype(y_ref.dtype)


def norm_mod_matmul(x2, g, scale, shift, w, S, *, emit_h=False, out_dtype=F32, tm=1024, tn=512):
    T, D = x2.shape
    N = w.shape[1]
    tm = min(tm, S)
    assert S % tm == 0 and N % tn == 0
    per = S // tm
    B = T // S
    out_shape = [jax.ShapeDtypeStruct((T, N), out_dtype)]
    out_specs = [pl.BlockSpec((tm, tn), lambda i, j: (i, j))]
    if emit_h:
        out_shape.append(jax.ShapeDtypeStruct((T, D), BF16))
        out_specs.append(pl.BlockSpec((tm, D), lambda i, j: (i, 0)))
    res = pl.pallas_call(
        functools.partial(_nmm_kernel, emit_h=emit_h),
        grid=(T // tm, N // tn),
        in_specs=[pl.BlockSpec((tm, D), lambda i, j: (i, 0)),
                  pl.BlockSpec((1, D), lambda i, j: (0, 0)),
                  pl.BlockSpec((1, 1, D), lambda i, j: (i // per, 0, 0)),
                  pl.BlockSpec((1, 1, D), lambda i, j: (i // per, 0, 0)),
                  pl.BlockSpec((D, tn), lambda i, j: (0, j))],
        out_specs=out_specs,
        out_shape=out_shape,
        scratch_shapes=[pltpu.VMEM((tm, D), BF16)],
        compiler_params=_cparams(("arbitrary", "arbitrary")),
        name="norm_mod_matmul",
    )(x2, g.reshape(1, D), scale.reshape(B, 1, D), shift.reshape(B, 1, D), w)
    return res if emit_h else res[0]


def _mmres_kernel(a_ref, w_ref, x_ref, gate_ref, o_ref):
    y = jnp.dot(a_ref[...], w_ref[...], preferred_element_type=F32)
    o_ref[...] = x_ref[...] + gate_ref[0] * y


def matmul_residual(a, w, x2, gate, S, *, tm=1024, tn=512):
    T, K = a.shape
    N = w.shape[1]
    tm = min(tm, S)
    per = S // tm
    B = T // S
    return pl.pallas_call(
        _mmres_kernel,
        grid=(T // tm, N // tn),
        in_specs=[pl.BlockSpec((tm, K), lambda i, j: (i, 0)),
                  pl.BlockSpec((K, tn), lambda i, j: (0, j)),
                  pl.BlockSpec((tm, tn), lambda i, j: (i, j)),
                  pl.BlockSpec((1, 1, tn), lambda i, j: (i // per, 0, j))],
        out_specs=pl.BlockSpec((tm, tn), lambda i, j: (i, j)),
        out_shape=jax.ShapeDtypeStruct((T, N), F32),
        compiler_params=_cparams(("arbitrary", "arbitrary")),
        name="matmul_residual",
    )(a, w, x2, gate.reshape(B, 1, N))


def _rope_block(v, cos, sin, hd):
    if hd == 64:
        lane = lax.broadcasted_iota(jnp.int32, v.shape, 1)
        lo = (lane % 64) < 32
        partner = jnp.where(lo, pltpu.roll(v, 96, 1), pltpu.roll(v, 32, 1))
    else:
        partner = pltpu.roll(v, 64, 1)
    return v * cos + partner * sin


def _rope_kernel(y_ref, cos_ref, sin_ref, o_ref, *, hd):
    cos = cos_ref[...]
    sin = sin_ref[...]
    for k in range(o_ref.shape[1] // LANES):
        sl = slice(k * LANES, (k + 1) * LANES)
        o_ref[:, sl] = _rope_block(y_ref[:, sl], cos, sin, hd).astype(o_ref.dtype)


def _cast_kernel(y_ref, o_ref):
    o_ref[...] = y_ref[...].astype(o_ref.dtype)


def rope_cols(y, col0, ncols, cos, sin, hd, S, *, tm=512, cw=512):
    T = y.shape[0]
    tm = min(tm, S)
    per = S // tm
    assert col0 % cw == 0 and ncols % cw == 0
    c0 = col0 // cw
    return pl.pallas_call(
        functools.partial(_rope_kernel, hd=hd),
        grid=(T // tm, ncols // cw),
        in_specs=[pl.BlockSpec((tm, cw), lambda i, j: (i, c0 + j)),
                  pl.BlockSpec((tm, LANES), lambda i, j: (i % per, 0)),
                  pl.BlockSpec((tm, LANES), lambda i, j: (i % per, 0))],
        out_specs=pl.BlockSpec((tm, cw), lambda i, j: (i, j)),
        out_shape=jax.ShapeDtypeStruct((T, ncols), BF16),
        compiler_params=_cparams(("arbitrary", "arbitrary")),
        name="rope_cols",
    )(y, cos, sin)


def cast_cols(y, col0, ncols, dtype, S, *, tm=512, cw=512):
    T = y.shape[0]
    tm = min(tm, S)
    cw = min(cw, ncols)
    assert col0 % cw == 0 and ncols % cw == 0
    c0 = col0 // cw
    return pl.pallas_call(
        _cast_kernel,
        grid=(T // tm, ncols // cw),
        in_specs=[pl.BlockSpec((tm, cw), lambda i, j: (i, c0 + j))],
        out_specs=pl.BlockSpec((tm, cw), lambda i, j: (i, j)),
        out_shape=jax.ShapeDtypeStruct((T, ncols), dtype),
        compiler_params=_cparams(("arbitrary", "arbitrary")),
        name="cast_cols",
    )(y)


def _rope_tables(pos, hd):
    inv = jnp.power(ROPE_THETA, -jnp.arange(0, hd, 2, dtype=F32) / hd)
    ang = pos.astype(F32)[:, None] * inv[None, :]
    cos, sin = jnp.cos(ang), jnp.sin(ang)
    reps = LANES // hd
    cos_l = jnp.tile(jnp.concatenate([cos, cos], axis=1), (1, reps))
    sin_l = jnp.tile(jnp.concatenate([-sin, sin], axis=1), (1, reps))
    return cos_l, sin_l


def _split_halves(q):
    lane = lax.broadcasted_iota(jnp.int32, q.shape, 1)
    zero = jnp.zeros_like(q)
    return jnp.concatenate([jnp.where(lane < 64, q, zero), jnp.where(lane >= 64, q, zero)], axis=0)


def _diff_kernel(q_ref, k_ref, v_ref, lv_ref, sg_ref, o_ref, *, tq, tk, lam_init):
    i = pl.program_id(2)
    q0 = i * tq
    qq = _split_halves(q_ref[...])
    rows = 2 * tq

    def step(c, carry, masked):
        m, l, acc = carry
        k0 = pl.multiple_of(c * tk, tk)
        kc = k_ref[pl.ds(k0, tk), :]
        vc = v_ref[pl.ds(k0, tk), :]
        s = lax.dot_general(qq, kc, _NT, preferred_element_type=F32)
        if masked:
            r = lax.broadcasted_iota(jnp.int32, (rows, tk), 0) % tq
            cc = lax.broadcasted_iota(jnp.int32, (rows, tk), 1)
            s = jnp.where(cc - r <= q0 - k0, s, NEG_INF)
        m_new = jnp.maximum(m, jnp.max(s, axis=-1, keepdims=True))
        alpha = jnp.exp(m - m_new)
        p = jnp.exp(s - m_new)
        l = alpha * l + jnp.sum(p, axis=-1, keepdims=True)
        acc = alpha * acc + jnp.dot(p.astype(BF16), vc, preferred_element_type=F32)
        return m_new, l, acc

    init = (jnp.full((rows, 1), NEG_INF, F32), jnp.zeros((rows, 1), F32), jnp.zeros((rows, LANES), F32))
    nfull = q0 // tk
    carry = lax.fori_loop(0, nfull, functools.partial(step, masked=False), init)
    m, l, acc = step(nfull, carry, True)
    o = acc / l
    lv = lv_ref[...]
    lam = (jnp.exp(jnp.sum(lv[0:1] * lv[1:2], axis=-1, keepdims=True))
           - jnp.exp(jnp.sum(lv[2:3] * lv[3:4], axis=-1, keepdims=True)) + lam_init)
    o = o[:tq] - lam * o[tq:]
    o = o * lax.rsqrt(jnp.mean(o * o, axis=-1, keepdims=True) + NORM_EPS) * sg_ref[...]
    o_ref[...] = (o * (1.0 - lam_init)).astype(o_ref.dtype)


def diff_attention(qk, v, lam_vecs, subln_g, B, S, lam_init, *, q_col0, k_col0, v_col0, tq=256, tk=1024):
    T = B * S
    tq = min(tq, S)
    tk = min(tk, S)
    assert tk % tq == 0 and S % tk == 0
    nq = S // tq
    H = DA_HEADS
    qb, kb, vb = q_col0 // LANES, k_col0 // LANES, v_col0 // LANES
    return pl.pallas_call(
        functools.partial(_diff_kernel, tq=tq, tk=tk, lam_init=lam_init),
        grid=(B, H, nq),
        in_specs=[pl.BlockSpec((tq, LANES), lambda b, h, i: (b * nq + i, qb + h)),
                  pl.BlockSpec((S, LANES), lambda b, h, i: (b, kb + h)),
                  pl.BlockSpec((S, LANES), lambda b, h, i: (b, vb + h)),
                  pl.BlockSpec((4, DA_DIM), lambda b, h, i: (0, 0)),
                  pl.BlockSpec((1, DA_VDIM), lambda b, h, i: (0, 0))],
        out_specs=pl.BlockSpec((tq, LANES), lambda b, h, i: (b * nq + i, h)),
        out_shape=jax.ShapeDtypeStruct((T, H * DA_VDIM), BF16),
        compiler_params=_cparams(("arbitrary", "arbitrary", "arbitrary")),
        name="diff_attention",
    )(qk, qk, v, lam_vecs.astype(F32), subln_g.reshape(1, DA_VDIM).astype(F32))


def _swa_kernel(sink_ref, q_ref, k_ref, v_ref, o_ref, *, tq, window):
    pr = pl.program_id(1)
    i = pl.program_id(2)
    q0 = i * tq
    span = tq + window
    ks = pl.multiple_of(jnp.maximum(q0 - window, 0), window)
    qq = _split_halves(q_ref[...])
    rows = 2 * tq
    kc = k_ref[pl.ds(ks, span), :]
    vc = v_ref[pl.ds(ks, span), :]
    s = lax.dot_general(qq, kc, _NT, preferred_element_type=F32)
    r = lax.broadcasted_iota(jnp.int32, (rows, span), 0) % tq
    cc = lax.broadcasted_iota(jnp.int32, (rows, span), 1)
    rel = (q0 - ks) + r - cc
    s = jnp.where((rel >= 0) & (rel < window), s, NEG_INF)
    rr = lax.broadcasted_iota(jnp.int32, (rows, 1), 0)
    sink = jnp.where(rr < tq, sink_ref[2 * pr], sink_ref[2 * pr + 1])
    m = jnp.maximum(jnp.max(s, axis=-1, keepdims=True), sink)
    p = jnp.exp(s - m)
    den = jnp.sum(p, axis=-1, keepdims=True) + jnp.exp(sink - m)
    o2 = jnp.dot(p.astype(BF16), vc, preferred_element_type=F32) / den
    lane = lax.broadcasted_iota(jnp.int32, (tq, LANES), 1)
    o_ref[...] = jnp.where(lane < 64, o2[:tq], o2[tq:]).astype(o_ref.dtype)


def swa_attention(qk, v, sinks, B, S, *, q_col0, k_col0, v_col0, tq=256):
    T = B * S
    tq = min(tq, S)
    nq = S // tq
    npair = SW_HEADS // 2
    ppk = (SW_HEADS // SW_KV) // 2
    qb, kb, vb = q_col0 // LANES, k_col0 // LANES, v_col0 // LANES
    return pl.pallas_call(
        functools.partial(_swa_kernel, tq=tq, window=SW_WINDOW),
        grid=(B, npair, nq),
        in_specs=[pl.BlockSpec(memory_space=pltpu.SMEM),
                  pl.BlockSpec((tq, LANES), lambda b, p, i: (b * nq + i, qb + p)),
                  pl.BlockSpec((S, LANES), lambda b, p, i: (b, kb + p // ppk)),
                  pl.BlockSpec((S, LANES), lambda b, p, i: (b, vb + p // ppk))],
        out_specs=pl.BlockSpec((tq, LANES), lambda b, p, i: (b * nq + i, p)),
        out_shape=jax.ShapeDtypeStruct((T, SW_HEADS * SW_DIM), BF16),
        compiler_params=_cparams(("arbitrary", "arbitrary", "arbitrary")),
        name="swa_attention",
    )(sinks.astype(F32), qk, qk, v)


def _compress_kernel(x_ref, pos_ref, w1a_ref, w1b_ref, w2_ref, cos_ref, sin_ref, o_ref, acc_a, acc_b):
    tok = pl.program_id(1)
    ntok = pl.num_programs(1)

    @pl.when(tok == 0)
    def _():
        acc_a[...] = jnp.zeros_like(acc_a)
        acc_b[...] = jnp.zeros_like(acc_b)

    for kv in range(2):
        pa = pos_ref[kv, pl.ds(tok, 1), :]
        pb = pos_ref[kv, pl.ds(CMP_STRIDE + tok, 1), :]
        for h in range(NSA_KV):
            idx = kv * NSA_KV + h
            xs = x_ref[0, :, idx * LANES:(idx + 1) * LANES]
            acc_a[idx] += jnp.dot((xs + pa).astype(BF16), w1a_ref[kv], preferred_element_type=F32)
            acc_b[idx] += jnp.dot((xs + pb).astype(BF16), w1b_ref[kv], preferred_element_type=F32)

    @pl.when(tok == ntok - 1)
    def _():
        ncp = acc_a.shape[1]
        for kv in range(2):
            for h in range(NSA_KV):
                idx = kv * NSA_KV + h
                hid = acc_a[idx] + pltpu.roll(acc_b[idx], ncp - 1, 0)
                out = jnp.dot(jax.nn.gelu(hid).astype(BF16), w2_ref[kv], preferred_element_type=F32)
                if kv == 0:
                    out = _rope_block(out, cos_ref[...], sin_ref[...], NSA_DIM)
                o_ref[0, idx] = out.astype(o_ref.dtype)


def nsa_compress(xc, cmp_pos, cmp_w1, cmp_w2, cos_c, sin_c, B, S):
    ncp = S // CMP_STRIDE
    width = 2 * NSA_KV * NSA_DIM
    x3 = xc.reshape(B, ncp, CMP_STRIDE * width)
    half = CMP_STRIDE * NSA_DIM
    w1 = cmp_w1.astype(BF16)
    return pl.pallas_call(
        _compress_kernel,
        grid=(B, CMP_STRIDE),
        in_specs=[pl.BlockSpec((1, ncp, width), lambda b, t: (b, 0, t)),
                  pl.BlockSpec((2, CMP_LEN, NSA_DIM), lambda b, t: (0, 0, 0)),
                  pl.BlockSpec((2, NSA_DIM, CMP_HIDDEN), lambda b, t: (0, t, 0)),
                  pl.BlockSpec((2, NSA_DIM, CMP_HIDDEN), lambda b, t: (0, CMP_STRIDE + t, 0)),
                  pl.BlockSpec((2, CMP_HIDDEN, NSA_DIM), lambda b, t: (0, 0, 0)),
                  pl.BlockSpec((ncp, LANES), lambda b, t: (0, 0)),
                  pl.BlockSpec((ncp, LANES), lambda b, t: (0, 0))],
        out_specs=pl.BlockSpec((1, 2 * NSA_KV, ncp, NSA_DIM), lambda b, t: (b, 0, 0, 0)),
        out_shape=jax.ShapeDtypeStruct((B, 2 * NSA_KV, ncp, NSA_DIM), BF16),
        scratch_shapes=[pltpu.VMEM((2 * NSA_KV, ncp, CMP_HIDDEN), F32),
                        pltpu.VMEM((2 * NSA_KV, ncp, CMP_HIDDEN), F32)],
        compiler_params=_cparams(("arbitrary", "arbitrary")),
        name="nsa_compress",
    )(x3, cmp_pos.astype(F32), w1, w1, cmp_w2.astype(BF16), cos_c, sin_c)


def _stack_heads(q_ref, tq):
    return jnp.concatenate([q_ref[:, g * LANES:(g + 1) * LANES] for g in range(NSA_G)], axis=0)


def _nsa_cmp_kernel(q_ref, kc_ref, vc_ref, ovt_ref, o_ref, sel_ref, *, tq, scale):
    i = pl.program_id(2)
    q0 = i * tq
    rows = NSA_G * tq
    ncp = kc_ref.shape[2]
    ns = ovt_ref.shape[0]
    qq = _stack_heads(q_ref, tq)
    s = lax.dot_general(qq, kc_ref[0, 0], _NT, preferred_element_type=F32) * scale
    tpos = q0 + lax.broadcasted_iota(jnp.int32, (rows, ncp), 0) % tq
    cend = lax.broadcasted_iota(jnp.int32, (rows, ncp), 1) * CMP_STRIDE + (CMP_LEN - 1)
    s = jnp.where(cend <= tpos, s, NEG_INF)
    m = jnp.max(s, axis=-1, keepdims=True)
    m = jnp.where(m == NEG_INF, 0.0, m)
    p = jnp.exp(s - m)
    p = p / jnp.maximum(jnp.sum(p, axis=-1, keepdims=True), 1e-30)
    o = jnp.dot(p.astype(BF16), vc_ref[0, 0], preferred_element_type=F32)
    for g in range(NSA_G):
        o_ref[:, g * LANES:(g + 1) * LANES] = o[g * tq:(g + 1) * tq]
    psum = p[0:tq]
    for g in range(1, NSA_G):
        psum = psum + p[g * tq:(g + 1) * tq]
    p_hi = psum.astype(BF16)
    p_lo = (psum - p_hi.astype(F32)).astype(BF16)
    ovt = ovt_ref[...]
    imp = (lax.dot_general(ovt, p_hi, _NT, preferred_element_type=F32)
           + lax.dot_general(ovt, p_lo, _NT, preferred_element_type=F32))
    blk = lax.broadcasted_iota(jnp.int32, (ns, tq), 0)
    cur = (q0 + lax.broadcasted_iota(jnp.int32, (ns, tq), 1)) // SLC_LEN
    future = blk > cur
    forced = (blk == 0) | (blk == cur) | (blk == cur - 1)
    imp = jnp.where(forced, jnp.inf, imp)
    imp = jnp.where(future, NEG_INF, imp)
    sub = 8
    groups = [imp[g * sub:(g + 1) * sub] for g in range(ns // sub)]
    cnts = [jnp.zeros((sub, tq), F32) for _ in groups]
    below = lax.broadcasted_iota(jnp.int32, (sub, tq), 0)
    for sp in range(ns):
        row = imp[sp:sp + 1]
        for g, vals in enumerate(groups):
            if (g + 1) * sub - 1 < sp:
                beats = row > vals
            elif g * sub > sp:
                beats = row >= vals
            else:
                beats = (row > vals) | ((row == vals) & (below > sp - g * sub))
            cnts[g] = cnts[g] + jnp.where(beats, 1.0, 0.0)
    cnt = jnp.concatenate(cnts, axis=0)
    sel_t = jnp.where((cnt < float(SLC_TOPN)) & jnp.logical_not(future), 1.0, 0.0)
    sel_ref[0, 0] = sel_t.T.astype(sel_ref.dtype)


def nsa_compressed(q, kvc, ovt, B, S, *, tq=128):
    T = B * S
    nq = S // tq
    ncp = S // CMP_STRIDE
    ns = S // SLC_LEN
    gw = NSA_G * NSA_DIM
    return pl.pallas_call(
        functools.partial(_nsa_cmp_kernel, tq=tq, scale=NSA_DIM ** -0.5),
        grid=(B, NSA_KV, nq),
        in_specs=[pl.BlockSpec((tq, gw), lambda b, h, i: (b * nq + i, h)),
                  pl.BlockSpec((1, 1, ncp, NSA_DIM), lambda b, h, i: (b, h, 0, 0)),
                  pl.BlockSpec((1, 1, ncp, NSA_DIM), lambda b, h, i: (b, NSA_KV + h, 0, 0)),
                  pl.BlockSpec((ns, ncp), lambda b, h, i: (0, 0))],
        out_specs=[pl.BlockSpec((tq, gw), lambda b, h, i: (b * nq + i, h)),
                   pl.BlockSpec((1, 1, tq, ns), lambda b, h, i: (b, h, i, 0))],
        out_shape=[jax.ShapeDtypeStruct((T, NSA_HEADS * NSA_DIM), F32),
                   jax.ShapeDtypeStruct((B, NSA_KV, S, ns), BF16)],
        compiler_params=_cparams(("arbitrary", "arbitrary", "arbitrary")),
        name="nsa_compressed",
    )(q, kvc, kvc, ovt)


def _nsa_sel_kernel(q_ref, k_ref, v_ref, sel_ref, o_ref, *, tq, tk, scale):
    i = pl.program_id(2)
    q0 = i * tq
    rows = NSA_G * tq
    ns = sel_ref.shape[3]
    qq = _stack_heads(q_ref, tq)
    sel = sel_ref[0, 0]

    blk_off = (lax.broadcasted_iota(jnp.int32, (ns, tk), 0)
               - lax.broadcasted_iota(jnp.int32, (ns, tk), 1) // SLC_LEN)
    key_off = lax.broadcasted_iota(jnp.int32, (tq, tk), 1) - lax.broadcasted_iota(jnp.int32, (tq, tk), 0)

    def step(c, carry):
        m, l, acc = carry
        k0 = pl.multiple_of(c * tk, tk)
        kc = k_ref[pl.ds(k0, tk), :]
        vc = v_ref[pl.ds(k0, tk), :]
        s = lax.dot_general(qq, kc, _NT, preferred_element_type=F32) * scale
        expand = jnp.where(blk_off == c * (tk // SLC_LEN), 1.0, 0.0).astype(BF16)
        picked = jnp.dot(sel, expand, preferred_element_type=F32)
        valid = (picked > 0.5) & (key_off <= q0 - k0)
        s3 = jnp.where(valid[None], s.reshape(NSA_G, tq, tk), NEG_INF)
        s = s3.reshape(rows, tk)
        m_new = jnp.maximum(m, jnp.max(s, axis=-1, keepdims=True))
        m_safe = jnp.where(m_new == NEG_INF, 0.0, m_new)
        alpha = jnp.exp(m - m_safe)
        p = jnp.exp(s - m_safe)
        l = alpha * l + jnp.sum(p, axis=-1, keepdims=True)
        acc = alpha * acc + jnp.dot(p.astype(BF16), vc, preferred_element_type=F32)
        return m_new, l, acc

    init = (jnp.full((rows, 1), NEG_INF, F32), jnp.zeros((rows, 1), F32), jnp.zeros((rows, LANES), F32))
    nchunk = (q0 + tq + tk - 1) // tk
    m, l, acc = lax.fori_loop(0, nchunk, step, init)
    o = acc / jnp.maximum(l, 1e-30)
    for g in range(NSA_G):
        o_ref[:, g * LANES:(g + 1) * LANES] = o[g * tq:(g + 1) * tq]


def nsa_selected(q, k, v, sel, B, S, *, k_col0, v_col0, tq=128, tk=1024):
    T = B * S
    tk = min(tk, S)
    assert S % tk == 0 and tk % SLC_LEN == 0
    nq = S // tq
    ns = S // SLC_LEN
    gw = NSA_G * NSA_DIM
    kb, vb = k_col0 // LANES, v_col0 // LANES
    return pl.pallas_call(
        functools.partial(_nsa_sel_kernel, tq=tq, tk=tk, scale=NSA_DIM ** -0.5),
        grid=(B, NSA_KV, nq),
        in_specs=[pl.BlockSpec((tq, gw), lambda b, h, i: (b * nq + i, h)),
                  pl.BlockSpec((S, LANES), lambda b, h, i: (b, kb + h)),
                  pl.BlockSpec((S, LANES), lambda b, h, i: (b, vb + h)),
                  pl.BlockSpec((1, 1, tq, ns), lambda b, h, i: (b, h, i, 0))],
        out_specs=pl.BlockSpec((tq, gw), lambda b, h, i: (b * nq + i, h)),
        out_shape=jax.ShapeDtypeStruct((T, NSA_HEADS * NSA_DIM), F32),
        compiler_params=_cparams(("arbitrary", "arbitrary", "arbitrary")),
        name="nsa_selected",
    )(q, k, v, sel)


def _nsa_win_kernel(q_ref, k_ref, v_ref, o_ref, *, tq, window, scale):
    i = pl.program_id(2)
    q0 = i * tq
    rows = NSA_G * tq
    span = tq + window
    ks = pl.multiple_of(jnp.maximum(q0 - window, 0), tq)
    qq = _stack_heads(q_ref, tq)
    kc = k_ref[pl.ds(ks, span), :]
    vc = v_ref[pl.ds(ks, span), :]
    s = lax.dot_general(qq, kc, _NT, preferred_element_type=F32) * scale
    r = lax.broadcasted_iota(jnp.int32, (rows, span), 0) % tq
    cc = lax.broadcasted_iota(jnp.int32, (rows, span), 1)
    rel = (q0 - ks) + r - cc
    s = jnp.where((rel >= 0) & (rel < window), s, NEG_INF)
    m = jnp.max(s, axis=-1, keepdims=True)
    p = jnp.exp(s - m)
    o = jnp.dot(p.astype(BF16), vc, preferred_element_type=F32) / jnp.sum(p, axis=-1, keepdims=True)
    for g in range(NSA_G):
        o_ref[:, g * LANES:(g + 1) * LANES] = o[g * tq:(g + 1) * tq]


def nsa_window(q, k, v, B, S, *, k_col0, v_col0, tq=128):
    T = B * S
    nq = S // tq
    gw = NSA_G * NSA_DIM
    kb, vb = k_col0 // LANES, v_col0 // LANES
    assert S >= tq + NSA_WINDOW
    return pl.pallas_call(
        functools.partial(_nsa_win_kernel, tq=tq, window=NSA_WINDOW, scale=NSA_DIM ** -0.5),
        grid=(B, NSA_KV, nq),
        in_specs=[pl.BlockSpec((tq, gw), lambda b, h, i: (b * nq + i, h)),
                  pl.BlockSpec((S, LANES), lambda b, h, i: (b, kb + h)),
                  pl.BlockSpec((S, LANES), lambda b, h, i: (b, vb + h))],
        out_specs=pl.BlockSpec((tq, gw), lambda b, h, i: (b * nq + i, h)),
        out_shape=jax.ShapeDtypeStruct((T, NSA_HEADS * NSA_DIM), F32),
        compiler_params=_cparams(("arbitrary", "arbitrary", "arbitrary")),
        name="nsa_window",
    )(q, k, v)


def _nsa_mix_kernel(g_ref, oc_ref, os_ref, ow_ref, o_ref):
    sg = jax.nn.sigmoid(g_ref[...])
    for hq in range(NSA_HEADS):
        sl = slice(hq * LANES, (hq + 1) * LANES)
        acc = sg[:, hq:hq + 1] * oc_ref[:, sl]
        acc = acc + sg[:, NSA_HEADS + hq:NSA_HEADS + hq + 1] * os_ref[:, sl]
        acc = acc + sg[:, 2 * NSA_HEADS + hq:2 * NSA_HEADS + hq + 1] * ow_ref[:, sl]
        o_ref[:, sl] = acc.astype(o_ref.dtype)


def nsa_mix(gates, oc, os_, ow, *, tm=256):
    T, N = oc.shape
    spec = pl.BlockSpec((tm, N), lambda i: (i, 0))
    return pl.pallas_call(
        _nsa_mix_kernel,
        grid=(T // tm,),
        in_specs=[pl.BlockSpec((tm, LANES), lambda i: (i, 0)), spec, spec, spec],
        out_specs=spec,
        out_shape=jax.ShapeDtypeStruct((T, N), BF16),
        compiler_params=_cparams(("arbitrary",)),
        name="nsa_mix",
    )(gates, oc, os_, ow)


NO_RANK = 127.0
GELU_C0 = math.sqrt(2.0 / math.pi)
GELU_C1 = 0.044715 * GELU_C0


def _extract_sorted(s, out_ref, idx_ref, n):
    rowi = lax.broadcasted_iota(jnp.int32, s.shape, 0).astype(F32)

    def body(a, s):
        m = jnp.max(s, axis=0, keepdims=True)
        idx = jnp.min(jnp.where(s == m, rowi, 1e9), axis=0, keepdims=True)
        out_ref[pl.ds(a, 1), :] = m
        idx_ref[pl.ds(a, 1), :] = idx
        return jnp.where(rowi == idx, NEG_INF, s)

    return lax.fori_loop(0, n, body, s)


def _peer_route_kernel(q_ref, sk_ref, n0_ref, e0_ref, r1_ref, e1_ref, sv01, ix01, best, ixc):
    K = PEER_TOPK
    q = q_ref[...]
    tt = q.shape[0]
    s0 = lax.dot_general(sk_ref[0, 0], q[:, :PEER_HALF], _NT, preferred_element_type=F32)
    s1 = lax.dot_general(sk_ref[0, 1], q[:, PEER_HALF:], _NT, preferred_element_type=F32)
    _extract_sorted(jnp.concatenate([s0, s1], axis=1), sv01, ix01, K)
    a0 = sv01[:, :tt]
    a1 = sv01[:, tt:]
    cand = jnp.concatenate([a0[0:1] + a1] + [a0[a:a + 1] + a1[0:8] for a in range(1, 8)]
                           + [a0[8:16] + a1[0:1]], axis=0)
    left = _extract_sorted(cand, best, ixc, K)
    taken = jnp.where(left == NEG_INF, 1.0, 0.0)
    bv = best[...]
    z = jnp.sum(jnp.exp(bv - bv[0:1]), axis=0, keepdims=True)
    tail = taken[72:80]
    cnts = ([jnp.sum(taken[0:16], axis=0, keepdims=True)]
            + [jnp.sum(taken[8 + 8 * a:16 + 8 * a], axis=0, keepdims=True) for a in range(1, 8)]
            + [tail[a - 8:a - 7] for a in range(8, 16)])
    rowi = lax.broadcasted_iota(jnp.int32, s0.shape, 0).astype(F32)
    n0 = jnp.zeros_like(s0)
    r1 = jnp.full(s1.shape, NO_RANK, F32)
    for a in range(K):
        n0 = jnp.where(rowi == ix01[a:a + 1, :tt], cnts[a], n0)
        r1 = jnp.where(rowi == ix01[a:a + 1, tt:], float(a), r1)
    n0_ref[0] = n0
    e0_ref[0] = jnp.exp(s0 - a0[0:1])
    r1_ref[0] = r1
    e1_ref[0] = 0.5 * jnp.exp(s1 - a1[0:1]) / z


def peer_route(q, subkeys, *, tt=256):
    T = q.shape[0]
    H = PEER_HEADS
    f32 = jax.ShapeDtypeStruct((H, PEER_NKEYS, T), F32)
    bspec = pl.BlockSpec((1, PEER_NKEYS, tt), lambda i, h: (h, 0, i))
    return pl.pallas_call(
        _peer_route_kernel,
        grid=(T // tt, H),
        in_specs=[pl.BlockSpec((tt, 2 * PEER_HALF), lambda i, h: (i, h)),
                  pl.BlockSpec((1, 2, PEER_NKEYS, PEER_HALF), lambda i, h: (h, 0, 0, 0))],
        out_specs=[bspec, bspec, bspec, bspec],
        out_shape=[f32, f32, f32, f32],
        scratch_shapes=[pltpu.VMEM((PEER_TOPK, 2 * tt), F32), pltpu.VMEM((PEER_TOPK, 2 * tt), F32),
                        pltpu.VMEM((PEER_TOPK, tt), F32), pltpu.VMEM((PEER_TOPK, tt), F32)],
        compiler_params=_cparams(("arbitrary", "arbitrary")),
        name="peer_route",
    )(q, subkeys.astype(BF16))


def _peer_mix_kernel(ht_ref, u_ref, vt_ref, n0_ref, e0_ref, r1_ref, e1_ref, x_ref,
                     gate_ref, o_ref, acc_ref, at0, at1, cf0, cf1, *, te):
    g = pl.program_id(1)
    tt = ht_ref.shape[1]
    nslab = te // PEER_NKEYS

    @pl.when(g == 0)
    def _():
        acc_ref[...] = jnp.zeros_like(acc_ref)

    tile_rows = 64

    def build(at_ref, cf_ref, row0):
        for ii in range(nslab):
            n0rows = [n0_ref[hh, pl.ds(row0 + ii, 1), :] for hh in range(PEER_HEADS)]
            e0rows = [e0_ref[hh, pl.ds(row0 + ii, 1), :] for hh in range(PEER_HEADS)]
            for ts in range(tt // LANES):
                lanes = slice(ts * LANES, (ts + 1) * LANES)
                for part in range(PEER_NKEYS // tile_rows):
                    jr = slice(part * tile_rows, (part + 1) * tile_rows)
                    er = slice(ii * PEER_NKEYS + part * tile_rows, ii * PEER_NKEYS + (part + 1) * tile_rows)
                    w = None
                    for hh in range(PEER_HEADS):
                        term = jnp.where(r1_ref[hh, jr, lanes] < n0rows[hh][:, lanes],
                                         e0rows[hh][:, lanes] * e1_ref[hh, jr, lanes], 0.0)
                        w = term if w is None else w + term
                    a = at_ref[er, lanes]
                    act = a + a * jnp.tanh(a * (GELU_C0 + GELU_C1 * (a * a)))
                    cf_ref[er, lanes] = (w * act).astype(BF16)

    for t, (at_ref, cf_ref) in enumerate(((at0, cf0), (at1, cf1))):
        er = slice(t * te, (t + 1) * te)
        at_ref[...] = jnp.dot(u_ref[er, :], ht_ref[...], preferred_element_type=F32)
        build(at_ref, cf_ref, (2 * g + t) * nslab)
        acc_ref[...] += jnp.dot(vt_ref[:, er], cf_ref[...], preferred_element_type=F32)

    @pl.when(g == pl.num_programs(1) - 1)
    def _():
        o_ref[...] = x_ref[...] + gate_ref[0] * acc_ref[...].T


def peer_mix(ht, u, vt, route, x2, gate, S, *, tt=512, te=512):
    D, T = ht.shape
    E = u.shape[0]
    H = PEER_HEADS
    per = S // tt
    B = T // S
    n0, e0, r1, e1 = route
    once = pl.Buffered(1)
    bspec = pl.BlockSpec((H, PEER_NKEYS, tt), lambda i, g: (0, 0, i), pipeline_mode=once)
    return pl.pallas_call(
        functools.partial(_peer_mix_kernel, te=te),
        grid=(T // tt, E // (2 * te)),
        in_specs=[pl.BlockSpec((D, tt), lambda i, g: (0, i)),
                  pl.BlockSpec((2 * te, D), lambda i, g: (g, 0)),
                  pl.BlockSpec((D, 2 * te), lambda i, g: (0, g)),
                  bspec, bspec, bspec, bspec,
                  pl.BlockSpec((tt, D), lambda i, g: (i, 0), pipeline_mode=once),
                  pl.BlockSpec((1, 1, D), lambda i, g: (i // per, 0, 0))],
        out_specs=pl.BlockSpec((tt, D), lambda i, g: (i, 0)),
        out_shape=jax.ShapeDtypeStruct((T, D), F32),
        scratch_shapes=[pltpu.VMEM((D, tt), F32), pltpu.VMEM((te, tt), F32), pltpu.VMEM((te, tt), F32),
                        pltpu.VMEM((te, tt), BF16), pltpu.VMEM((te, tt), BF16)],
        compiler_params=_cparams(("arbitrary", "arbitrary"), vmem_mb=56),
        name="peer_mix",
    )(ht, u, vt, n0, e0, r1, e1, x2, gate.reshape(B, 1, D))


def _rmsnorm_kernel(x_ref, g_ref, o_ref):
    x = x_ref[...]
    ms = jnp.mean(x * x, axis=-1, keepdims=True)
    o_ref[...] = x * lax.rsqrt(ms + NORM_EPS) * g_ref[...]


def rmsnorm_rows(x2, g, *, tm=512):
    T, D = x2.shape
    return pl.pallas_call(
        _rmsnorm_kernel,
        grid=(T // tm,),
        in_specs=[pl.BlockSpec((tm, D), lambda i: (i, 0)), pl.BlockSpec((1, D), lambda i: (0, 0))],
        out_specs=pl.BlockSpec((tm, D), lambda i: (i, 0)),
        out_shape=jax.ShapeDtypeStruct((T, D), F32),
        compiler_params=_cparams(("arbitrary",)),
        name="final_rmsnorm",
    )(x2, g.reshape(1, D))


def _even_w_in_layout(w):
    D = w.shape[0]
    da = DA_HEADS * DA_DIM
    qa = w[:, 0:2 * da].reshape(D, 2, DA_HEADS, DA_DIM)
    ka = w[:, 2 * da:4 * da].reshape(D, 2, DA_HEADS, DA_DIM)
    o = 4 * da
    va = w[:, o:o + DA_HEADS * DA_VDIM]
    o += DA_HEADS * DA_VDIM
    qs = w[:, o:o + SW_HEADS * SW_DIM]
    o += SW_HEADS * SW_DIM
    ks = w[:, o:o + SW_KV * SW_DIM].reshape(D, SW_KV, 1, SW_DIM)
    o += SW_KV * SW_DIM
    vs = w[:, o:o + SW_KV * SW_DIM].reshape(D, SW_KV, 1, SW_DIM)
    scale = DA_DIM ** -0.5
    assert scale == 0.125 and SW_DIM == DA_DIM
    qa2 = (qa * scale).transpose(0, 2, 1, 3).reshape(D, 2 * da)
    ka2 = ka.transpose(0, 2, 1, 3).reshape(D, 2 * da)
    ks2 = jnp.broadcast_to(ks, (D, SW_KV, 2, SW_DIM)).reshape(D, 2 * SW_KV * SW_DIM)
    vs2 = jnp.broadcast_to(vs, (D, SW_KV, 2, SW_DIM)).reshape(D, 2 * SW_KV * SW_DIM)
    return jnp.concatenate([qa2, ka2, qs * scale, ks2, va, vs2], axis=1).astype(BF16)


def _odd_w_in_layout(w):
    D = w.shape[0]
    kvw = NSA_KV * NSA_DIM
    o = NSA_HEADS * NSA_DIM
    q = w[:, :o]
    kc, vc, ksl, vsl, kw, vw = [w[:, o + n * kvw:o + (n + 1) * kvw] for n in range(6)]
    gates = w[:, o + 6 * kvw:]
    pad = jnp.zeros((D, 3840 - (o + 6 * kvw + gates.shape[1])), w.dtype)
    return jnp.concatenate([q, ksl, kw, vsl, vw, kc, vc, gates, pad], axis=1).astype(BF16)


def _overlap_t(S):
    ncp = S // CMP_STRIDE
    ns = S // SLC_LEN
    cst = np.arange(ncp)[None, :] * CMP_STRIDE
    sst = np.arange(ns)[:, None] * SLC_LEN
    ov = np.clip(np.minimum(cst + CMP_LEN, sst + SLC_LEN) - np.maximum(cst, sst), 0, None)
    return jnp.asarray(ov.astype(np.float32) / CMP_LEN, dtype=BF16)


def _peer_layer(x2, mod, g, wq, subkeys, u_tab, v_tab, B, S):
    D = x2.shape[1]
    shift, scale, gate = mod[:, :D], mod[:, D:2 * D], mod[:, 2 * D:]
    q, h = norm_mod_matmul(x2, g, scale, shift, wq.astype(BF16), S, emit_h=True, out_dtype=BF16)
    route = peer_route(q, subkeys)
    return peer_mix(h.T, u_tab.astype(BF16), v_tab.astype(BF16).T, route, x2, gate, S)


def _even_layer(x2, mod, g, w_in, w_out, lam_vecs, subln_g, sinks, lam_init, cos64, sin64, B, S):
    D = x2.shape[1]
    shift, scale, gate = mod[:, :D], mod[:, D:2 * D], mod[:, 2 * D:]
    y = norm_mod_matmul(x2, g, scale, shift, _even_w_in_layout(w_in), S)
    qk = rope_cols(y, 0, 3584, cos64, sin64, 64, S)
    v = cast_cols(y, 3584, 1536, BF16, S)
    oa = diff_attention(qk, v, lam_vecs, subln_g, B, S, lam_init, q_col0=0, k_col0=1024, v_col0=0)
    ob = swa_attention(qk, v, sinks, B, S, q_col0=2048, k_col0=3072, v_col0=1024)
    o = jnp.concatenate([oa, ob], axis=1)
    return matmul_residual(o, w_out.astype(BF16), x2, gate, S)


def _odd_layer(x2, mod, g, w_in, w_out, cmp_pos, cmp_w1, cmp_w2, cos128, sin128, cos_c, sin_c, B, S):
    D = x2.shape[1]
    shift, scale, gate = mod[:, :D], mod[:, D:2 * D], mod[:, 2 * D:]
    y = norm_mod_matmul(x2, g, scale, shift, _odd_w_in_layout(w_in), S, tn=768)
    qk = rope_cols(y, 0, 2560, cos128, sin128, 128, S)
    v = cast_cols(y, 2560, 512, BF16, S)
    xc = cast_cols(y, 3072, 512, F32, S)
    gates = cast_cols(y, 3584, 128, F32, S, cw=128)
    kvc = nsa_compress(xc, cmp_pos, cmp_w1, cmp_w2, cos_c, sin_c, B, S)
    oc, sel = nsa_compressed(qk, kvc, _overlap_t(S), B, S)
    os_ = nsa_selected(qk, qk, v, sel, B, S, k_col0=2048, v_col0=0)
    ow = nsa_window(qk, qk, v, B, S, k_col0=2304, v_col0=256)
    o = nsa_mix(gates, oc, os_, ow)
    return matmul_residual(o, w_out.astype(BF16), x2, gate, S)


def kernel(x, c, ada_w, ada_b, norm_g, even_w_in, even_w_out, da_lambda, da_subln, sw_sinks, odd_w_in, odd_w_out, nsa_cmp_pos, nsa_cmp_w1, nsa_cmp_w2, peer_wq, peer_subkeys, peer_u, peer_v, final_g):
    B, S, D = x.shape
    depth = ada_w.shape[0]
    pos = jnp.arange(S)
    cos64, sin64 = _rope_tables(pos, DA_DIM)
    cos128, sin128 = _rope_tables(pos, NSA_DIM)
    cos_c, sin_c = _rope_tables(jnp.arange(S // CMP_STRIDE) * CMP_STRIDE + CMP_LEN - 1, NSA_DIM)
    mods = ada_mod(c, ada_w, ada_b)
    x2 = x.reshape(B * S, D)
    for layer in range(depth):
        j = layer // 2
        if layer % 2 == 0:
            lam_init = 0.8 - 0.6 * math.exp(-0.3 * layer)
            x2 = _even_layer(x2, mods[2 * layer], norm_g[layer, 0], even_w_in[j], even_w_out[j],
                             da_lambda[j], da_subln[j], sw_sinks[j], lam_init, cos64, sin64, B, S)
        else:
            x2 = _odd_layer(x2, mods[2 * layer], norm_g[layer, 0], odd_w_in[j], odd_w_out[j],
                            nsa_cmp_pos[j], nsa_cmp_w1[j], nsa_cmp_w2[j], cos128, sin128, cos_c, sin_c, B, S)
        x2 = _peer_layer(x2, mods[2 * layer + 1], norm_g[layer, 1], peer_wq[layer], peer_subkeys[layer],
                         peer_u[layer], peer_v[layer], B, S)
    return rmsnorm_rows(x2, final_g).reshape(B, S, D)
```

```python
import functools
import math

import jax
import jax.numpy as jnp
import numpy as np
from jax import lax
from jax.experimental import pallas as pl
from jax.experimental.pallas import tpu as pltpu

F32 = jnp.float32
BF16 = jnp.bfloat16
NEG_INF = float("-inf")

D_MODEL = 2048
ROPE_THETA = 10000.0
NORM_EPS = 1e-6

DA_HEADS = 8
DA_DIM = 64
DA_VDIM = 128
SW_HEADS = 16
SW_KV = 4
SW_DIM = 64
SW_WINDOW = 128

NSA_HEADS = 16
NSA_KV = 2
NSA_G = NSA_HEADS // NSA_KV
NSA_DIM = 128
CMP_LEN = 32
CMP_STRIDE = 16
CMP_HIDDEN = 256
SLC_LEN = 64
SLC_TOPN = 16
NSA_WINDOW = 512

PEER_HEADS = 8
PEER_NKEYS = 128
PEER_TOPK = 16
PEER_HALF = 128

LANES = 128
V7X_VMEM_BYTES = 64 * 1024 * 1024

_NT = (((1,), (1,)), ((), ()))


def _cparams(sem, vmem_mb=48):
    assert vmem_mb * 1024 * 1024 < V7X_VMEM_BYTES
    return pltpu.CompilerParams(dimension_semantics=sem, vmem_limit_bytes=vmem_mb * 1024 * 1024)


def _ada_kernel(c_ref, w_ref, b_ref, o_ref):
    c = c_ref[...]
    sc = c * jax.nn.sigmoid(c)
    o_ref[0] = jnp.dot(sc, w_ref[0], preferred_element_type=F32,
                       precision=lax.Precision.HIGHEST) + b_ref[0]


def ada_mod(c, ada_w, ada_b):
    B, D = c.shape
    n = ada_w.shape[0] * ada_w.shape[1]
    w = ada_w.reshape(n, D, 3 * D)
    b = ada_b.reshape(n, 1, 3 * D)
    rows = 8
    cp = jnp.zeros((rows, D), F32).at[:B].set(c)
    tn = 512
    out = pl.pallas_call(
        _ada_kernel,
        grid=(n, 3 * D // tn),
        in_specs=[pl.BlockSpec((rows, D), lambda l, j: (0, 0)),
                  pl.BlockSpec((1, D, tn), lambda l, j: (l, 0, j)),
                  pl.BlockSpec((1, 1, tn), lambda l, j: (l, 0, j))],
        out_specs=pl.BlockSpec((1, rows, tn), lambda l, j: (l, 0, j)),
        out_shape=jax.ShapeDtypeStruct((n, rows, 3 * D), F32),
        compiler_params=_cparams(("arbitrary", "arbitrary")),
        name="ada_mod",
    )(cp, w, b)
    return out[:, :B]


def _nmm_kernel(x_ref, g_ref, sc_ref, sh_ref, w_ref, *rest, emit_h):
    if emit_h:
        y_ref, h_out_ref, hs_ref = rest
    else:
        y_ref, hs_ref = rest

    @pl.when(pl.program_id(1) == 0)
    def _():
        x = x_ref[...]
        ms = jnp.mean(x * x, axis=-1, keepdims=True)
        y = x * lax.rsqrt(ms + NORM_EPS) * g_ref[...]
        h = (y * (1.0 + sc_ref[0]) + sh_ref[0]).astype(BF16)
        hs_ref[...] = h
        if emit_h:
            h_out_ref[...] = h

    y_ref[...] = jnp.dot(hs_ref[...], w_ref[...], preferred_element_type=F32).astype(y_ref.dtype)


def norm_mod_matmul(x2, g, scale, shift, w, S, *, emit_h=False, out_dtype=F32, tm=1024, tn=1024):
    T, D = x2.shape
    N = w.shape[1]
    tm = min(tm, S)
    assert S % tm == 0 and N % tn == 0
    per = S // tm
    B = T // S
    out_shape = [jax.ShapeDtypeStruct((T, N), out_dtype)]
    out_specs = [pl.BlockSpec((tm, tn), lambda i, j: (i, j))]
    if emit_h:
        out_shape.append(jax.ShapeDtypeStruct((T, D), BF16))
        out_specs.append(pl.BlockSpec((tm, D), lambda i, j: (i, 0)))
    res = pl.pallas_call(
        functools.partial(_nmm_kernel, emit_h=emit_h),
        grid=(T // tm, N // tn),
        in_specs=[pl.BlockSpec((tm, D), lambda i, j: (i, 0)),
                  pl.BlockSpec((1, D), lambda i, j: (0, 0)),
                  pl.BlockSpec((1, 1, D), lambda i, j: (i // per, 0, 0)),
                  pl.BlockSpec((1, 1, D), lambda i, j: (i // per, 0, 0)),
                  pl.BlockSpec((D, tn), lambda i, j: (0, j))],
        out_specs=out_specs,
        out_shape=out_shape,
        scratch_shapes=[pltpu.VMEM((tm, D), BF16)],
        compiler_params=_cparams(("arbitrary", "arbitrary"), vmem_mb=56),
        name="norm_mod_matmul",
    )(x2, g.reshape(1, D), scale.reshape(B, 1, D), shift.reshape(B, 1, D), w)
    return res if emit_h else res[0]


def _mmres_kernel(a_ref, w_ref, x_ref, gate_ref, o_ref):
    y = jnp.dot(a_ref[...], w_ref[...], preferred_element_type=F32)
    o_ref[...] = x_ref[...] + gate_ref[0] * y


def matmul_residual(a, w, x2, gate, S, *, tm=1024, tn=1024):
    T, K = a.shape
    N = w.shape[1]
    tm = min(tm, S)
    per = S // tm
    B = T // S
    return pl.pallas_call(
        _mmres_kernel,
        grid=(T // tm, N // tn),
        in_specs=[pl.BlockSpec((tm, K), lambda i, j: (i, 0)),
                  pl.BlockSpec((K, tn), lambda i, j: (0, j)),
                  pl.BlockSpec((tm, tn), lambda i, j: (i, j)),
                  pl.BlockSpec((1, 1, tn), lambda i, j: (i // per, 0, j))],
        out_specs=pl.BlockSpec((tm, tn), lambda i, j: (i, j)),
        out_shape=jax.ShapeDtypeStruct((T, N), F32),
        compiler_params=_cparams(("arbitrary", "arbitrary")),
        name="matmul_residual",
    )(a, w, x2, gate.reshape(B, 1, N))


def _rope_block(v, cos, sin, hd):
    if hd == 64:
        lane = lax.broadcasted_iota(jnp.int32, v.shape, 1)
        lo = (lane % 64) < 32
        partner = jnp.where(lo, pltpu.roll(v, 96, 1), pltpu.roll(v, 32, 1))
    else:
        partner = pltpu.roll(v, 64, 1)
    return v * cos + partner * sin


def _rope_kernel(y_ref, cos_ref, sin_ref, o_ref, *, hd):
    cos = cos_ref[...]
    sin = sin_ref[...]
    for k in range(o_ref.shape[1] // LANES):
        sl = slice(k * LANES, (k + 1) * LANES)
        o_ref[:, sl] = _rope_block(y_ref[:, sl], cos, sin, hd).astype(o_ref.dtype)


def _cast_kernel(y_ref, o_ref):
    o_ref[...] = y_ref[...].astype(o_ref.dtype)


def rope_cols(y, col0, ncols, cos, sin, hd, S, *, tm=512, cw=512):
    T = y.shape[0]
    tm = min(tm, S)
    per = S // tm
    assert col0 % cw == 0 and ncols % cw == 0
    c0 = col0 // cw
    return pl.pallas_call(
        functools.partial(_rope_kernel, hd=hd),
        grid=(T // tm, ncols // cw),
        in_specs=[pl.BlockSpec((tm, cw), lambda i, j: (i, c0 + j)),
                  pl.BlockSpec((tm, LANES), lambda i, j: (i % per, 0)),
                  pl.BlockSpec((tm, LANES), lambda i, j: (i % per, 0))],
        out_specs=pl.BlockSpec((tm, cw), lambda i, j: (i, j)),
        out_shape=jax.ShapeDtypeStruct((T, ncols), BF16),
        compiler_params=_cparams(("arbitrary", "arbitrary")),
        name="rope_cols",
    )(y, cos, sin)


def cast_cols(y, col0, ncols, dtype, S, *, tm=512, cw=512):
    T = y.shape[0]
    tm = min(tm, S)
    cw = min(cw, ncols)
    assert col0 % cw == 0 and ncols % cw == 0
    c0 = col0 // cw
    return pl.pallas_call(
        _cast_kernel,
        grid=(T // tm, ncols // cw),
        in_specs=[pl.BlockSpec((tm, cw), lambda i, j: (i, c0 + j))],
        out_specs=pl.BlockSpec((tm, cw), lambda i, j: (i, j)),
        out_shape=jax.ShapeDtypeStruct((T, ncols), dtype),
        compiler_params=_cparams(("arbitrary", "arbitrary")),
        name="cast_cols",
    )(y)


def _rope_tables(pos, hd):
    inv = jnp.power(ROPE_THETA, -jnp.arange(0, hd, 2, dtype=F32) / hd)
    ang = pos.astype(F32)[:, None] * inv[None, :]
    cos, sin = jnp.cos(ang), jnp.sin(ang)
    reps = LANES // hd
    cos_l = jnp.tile(jnp.concatenate([cos, cos], axis=1), (1, reps))
    sin_l = jnp.tile(jnp.concatenate([-sin, sin], axis=1), (1, reps))
    return cos_l, sin_l


def _split_halves(q):
    lane = lax.broadcasted_iota(jnp.int32, q.shape, 1)
    zero = jnp.zeros_like(q)
    return jnp.concatenate([jnp.where(lane < 64, q, zero), jnp.where(lane >= 64, q, zero)], axis=0)


def _diff_kernel(q_ref, k_ref, v_ref, lv_ref, sg_ref, o_ref, *, tq, tk, lam_init):
    i = pl.program_id(2)
    q0 = i * tq
    qq = _split_halves(q_ref[...])
    rows = 2 * tq

    def step(c, carry, masked):
        m, l, acc = carry
        k0 = pl.multiple_of(c * tk, tk)
        kc = k_ref[pl.ds(k0, tk), :]
        vc = v_ref[pl.ds(k0, tk), :]
        s = lax.dot_general(qq, kc, _NT, preferred_element_type=F32)
        if masked:
            r = lax.broadcasted_iota(jnp.int32, (rows, tk), 0) % tq
            cc = lax.broadcasted_iota(jnp.int32, (rows, tk), 1)
            s = jnp.where(cc - r <= q0 - k0, s, NEG_INF)
        m_new = jnp.maximum(m, jnp.max(s, axis=-1, keepdims=True))
        alpha = jnp.exp(m - m_new)
        p = jnp.exp(s - m_new)
        l = alpha * l + jnp.sum(p, axis=-1, keepdims=True)
        acc = alpha * acc + jnp.dot(p.astype(BF16), vc, preferred_element_type=F32)
        return m_new, l, acc

    init = (jnp.full((rows, 1), NEG_INF, F32), jnp.zeros((rows, 1), F32), jnp.zeros((rows, LANES), F32))
    nfull = q0 // tk
    carry = lax.fori_loop(0, nfull, functools.partial(step, masked=False), init)
    m, l, acc = step(nfull, carry, True)
    o = acc / l
    lv = lv_ref[...]
    lam = (jnp.exp(jnp.sum(lv[0:1] * lv[1:2], axis=-1, keepdims=True))
           - jnp.exp(jnp.sum(lv[2:3] * lv[3:4], axis=-1, keepdims=True)) + lam_init)
    o = o[:tq] - lam * o[tq:]
    o = o * lax.rsqrt(jnp.mean(o * o, axis=-1, keepdims=True) + NORM_EPS) * sg_ref[...]
    o_ref[...] = (o * (1.0 - lam_init)).astype(o_ref.dtype)


def diff_attention(qk, v, lam_vecs, subln_g, B, S, lam_init, *, q_col0, k_col0, v_col0, tq=256, tk=1024):
    T = B * S
    tq = min(tq, S)
    tk = min(tk, S)
    assert tk % tq == 0 and S % tk == 0
    nq = S // tq
    H = DA_HEADS
    qb, kb, vb = q_col0 // LANES, k_col0 // LANES, v_col0 // LANES
    return pl.pallas_call(
        functools.partial(_diff_kernel, tq=tq, tk=tk, lam_init=lam_init),
        grid=(B, H, nq),
        in_specs=[pl.BlockSpec((tq, LANES), lambda b, h, i: (b * nq + i, qb + h)),
                  pl.BlockSpec((S, LANES), lambda b, h, i: (b, kb + h)),
                  pl.BlockSpec((S, LANES), lambda b, h, i: (b, vb + h)),
                  pl.BlockSpec((4, DA_DIM), lambda b, h, i: (0, 0)),
                  pl.BlockSpec((1, DA_VDIM), lambda b, h, i: (0, 0))],
        out_specs=pl.BlockSpec((tq, LANES), lambda b, h, i: (b * nq + i, h)),
        out_shape=jax.ShapeDtypeStruct((T, H * DA_VDIM), BF16),
        compiler_params=_cparams(("arbitrary", "arbitrary", "arbitrary")),
        name="diff_attention",
    )(qk, qk, v, lam_vecs.astype(F32), subln_g.reshape(1, DA_VDIM).astype(F32))


def _swa_kernel(sink_ref, q_ref, k_ref, v_ref, o_ref, *, tq, window):
    pr = pl.program_id(1)
    i = pl.program_id(2)
    q0 = i * tq
    span = tq + window
    ks = pl.multiple_of(jnp.maximum(q0 - window, 0), window)
    qq = _split_halves(q_ref[...])
    rows = 2 * tq
    kc = k_ref[pl.ds(ks, span), :]
    vc = v_ref[pl.ds(ks, span), :]
    s = lax.dot_general(qq, kc, _NT, preferred_element_type=F32)
    r = lax.broadcasted_iota(jnp.int32, (rows, span), 0) % tq
    cc = lax.broadcasted_iota(jnp.int32, (rows, span), 1)
    rel = (q0 - ks) + r - cc
    s = jnp.where((rel >= 0) & (rel < window), s, NEG_INF)
    rr = lax.broadcasted_iota(jnp.int32, (rows, 1), 0)
    sink = jnp.where(rr < tq, sink_ref[2 * pr], sink_ref[2 * pr + 1])
    m = jnp.maximum(jnp.max(s, axis=-1, keepdims=True), sink)
    p = jnp.exp(s - m)
    den = jnp.sum(p, axis=-1, keepdims=True) + jnp.exp(sink - m)
    o2 = jnp.dot(p.astype(BF16), vc, preferred_element_type=F32) / den
    lane = lax.broadcasted_iota(jnp.int32, (tq, LANES), 1)
    o_ref[...] = jnp.where(lane < 64, o2[:tq], o2[tq:]).astype(o_ref.dtype)


def swa_attention(qk, v, sinks, B, S, *, q_col0, k_col0, v_col0, tq=256):
    T = B * S
    tq = min(tq, S)
    nq = S // tq
    npair = SW_HEADS // 2
    ppk = (SW_HEADS // SW_KV) // 2
    qb, kb, vb = q_col0 // LANES, k_col0 // LANES, v_col0 // LANES
    return pl.pallas_call(
        functools.partial(_swa_kernel, tq=tq, window=SW_WINDOW),
        grid=(B, npair, nq),
        in_specs=[pl.BlockSpec(memory_space=pltpu.SMEM),
                  pl.BlockSpec((tq, LANES), lambda b, p, i: (b * nq + i, qb + p)),
                  pl.BlockSpec((S, LANES), lambda b, p, i: (b, kb + p // ppk)),
                  pl.BlockSpec((S, LANES), lambda b, p, i: (b, vb + p // ppk))],
        out_specs=pl.BlockSpec((tq, LANES), lambda b, p, i: (b * nq + i, p)),
        out_shape=jax.ShapeDtypeStruct((T, SW_HEADS * SW_DIM), BF16),
        compiler_params=_cparams(("arbitrary", "arbitrary", "arbitrary")),
        name="swa_attention",
    )(sinks.astype(F32), qk, qk, v)


def _compress_kernel(x_ref, pos_ref, w1a_ref, w1b_ref, w2_ref, cos_ref, sin_ref, o_ref, acc_a, acc_b):
    tok = pl.program_id(1)
    ntok = pl.num_programs(1)

    @pl.when(tok == 0)
    def _():
        acc_a[...] = jnp.zeros_like(acc_a)
        acc_b[...] = jnp.zeros_like(acc_b)

    for kv in range(2):
        pa = pos_ref[kv, pl.ds(tok, 1), :]
        pb = pos_ref[kv, pl.ds(CMP_STRIDE + tok, 1), :]
        for h in range(NSA_KV):
            idx = kv * NSA_KV + h
            xs = x_ref[0, :, idx * LANES:(idx + 1) * LANES]
            acc_a[idx] += jnp.dot((xs + pa).astype(BF16), w1a_ref[kv], preferred_element_type=F32)
            acc_b[idx] += jnp.dot((xs + pb).astype(BF16), w1b_ref[kv], preferred_element_type=F32)

    @pl.when(tok == ntok - 1)
    def _():
        ncp = acc_a.shape[1]
        for kv in range(2):
            for h in range(NSA_KV):
                idx = kv * NSA_KV + h
                hid = acc_a[idx] + pltpu.roll(acc_b[idx], ncp - 1, 0)
                out = jnp.dot(jax.nn.gelu(hid).astype(BF16), w2_ref[kv], preferred_element_type=F32)
                if kv == 0:
                    out = _rope_block(out, cos_ref[...], sin_ref[...], NSA_DIM)
                o_ref[0, idx] = out.astype(o_ref.dtype)


def nsa_compress(xc, cmp_pos, cmp_w1, cmp_w2, cos_c, sin_c, B, S):
    ncp = S // CMP_STRIDE
    width = 2 * NSA_KV * NSA_DIM
    x3 = xc.reshape(B, ncp, CMP_STRIDE * width)
    half = CMP_STRIDE * NSA_DIM
    w1 = cmp_w1.astype(BF16)
    return pl.pallas_call(
        _compress_kernel,
        grid=(B, CMP_STRIDE),
        in_specs=[pl.BlockSpec((1, ncp, width), lambda b, t: (b, 0, t)),
                  pl.BlockSpec((2, CMP_LEN, NSA_DIM), lambda b, t: (0, 0, 0)),
                  pl.BlockSpec((2, NSA_DIM, CMP_HIDDEN), lambda b, t: (0, t, 0)),
                  pl.BlockSpec((2, NSA_DIM, CMP_HIDDEN), lambda b, t: (0, CMP_STRIDE + t, 0)),
                  pl.BlockSpec((2, CMP_HIDDEN, NSA_DIM), lambda b, t: (0, 0, 0)),
                  pl.BlockSpec((ncp, LANES), lambda b, t: (0, 0)),
                  pl.BlockSpec((ncp, LANES), lambda b, t: (0, 0))],
        out_specs=pl.BlockSpec((1, 2 * NSA_KV, ncp, NSA_DIM), lambda b, t: (b, 0, 0, 0)),
        out_shape=jax.ShapeDtypeStruct((B, 2 * NSA_KV, ncp, NSA_DIM), BF16),
        scratch_shapes=[pltpu.VMEM((2 * NSA_KV, ncp, CMP_HIDDEN), F32),
                        pltpu.VMEM((2 * NSA_KV, ncp, CMP_HIDDEN), F32)],
        compiler_params=_cparams(("arbitrary", "arbitrary")),
        name="nsa_compress",
    )(x3, cmp_pos.astype(F32), w1, w1, cmp_w2.astype(BF16), cos_c, sin_c)


def _stack_heads(q_ref, tq):
    return jnp.concatenate([q_ref[:, g * LANES:(g + 1) * LANES] for g in range(NSA_G)], axis=0)


def _nsa_cmp_kernel(q_ref, kc_ref, vc_ref, ovt_ref, o_ref, sel_ref, *, tq, scale):
    i = pl.program_id(2)
    q0 = i * tq
    rows = NSA_G * tq
    ncp = kc_ref.shape[2]
    ns = ovt_ref.shape[0]
    qq = _stack_heads(q_ref, tq)
    s = lax.dot_general(qq, kc_ref[0, 0], _NT, preferred_element_type=F32) * scale
    tpos = q0 + lax.broadcasted_iota(jnp.int32, (rows, ncp), 0) % tq
    cend = lax.broadcasted_iota(jnp.int32, (rows, ncp), 1) * CMP_STRIDE + (CMP_LEN - 1)
    s = jnp.where(cend <= tpos, s, NEG_INF)
    m = jnp.max(s, axis=-1, keepdims=True)
    m = jnp.where(m == NEG_INF, 0.0, m)
    p = jnp.exp(s - m)
    p = p / jnp.maximum(jnp.sum(p, axis=-1, keepdims=True), 1e-30)
    o = jnp.dot(p.astype(BF16), vc_ref[0, 0], preferred_element_type=F32)
    for g in range(NSA_G):
        o_ref[:, g * LANES:(g + 1) * LANES] = o[g * tq:(g + 1) * tq]
    psum = p[0:tq]
    for g in range(1, NSA_G):
        psum = psum + p[g * tq:(g + 1) * tq]
    p_hi = psum.astype(BF16)
    p_lo = (psum - p_hi.astype(F32)).astype(BF16)
    ovt = ovt_ref[...]
    imp = (lax.dot_general(ovt, p_hi, _NT, preferred_element_type=F32)
           + lax.dot_general(ovt, p_lo, _NT, preferred_element_type=F32))
    blk = lax.broadcasted_iota(jnp.int32, (ns, tq), 0)
    cur = (q0 + lax.broadcasted_iota(jnp.int32, (ns, tq), 1)) // SLC_LEN
    future = blk > cur
    forced = (blk == 0) | (blk == cur) | (blk == cur - 1)
    imp = jnp.where(forced, jnp.inf, imp)
    imp = jnp.where(future, NEG_INF, imp)
    sub = 8
    groups = [imp[g * sub:(g + 1) * sub] for g in range(ns // sub)]
    cnts = [jnp.zeros((sub, tq), F32) for _ in groups]
    below = lax.broadcasted_iota(jnp.int32, (sub, tq), 0)
    for sp in range(ns):
        row = imp[sp:sp + 1]
        for g, vals in enumerate(groups):
            if (g + 1) * sub - 1 < sp:
                beats = row > vals
            elif g * sub > sp:
                beats = row >= vals
            else:
                beats = (row > vals) | ((row == vals) & (below > sp - g * sub))
            cnts[g] = cnts[g] + jnp.where(beats, 1.0, 0.0)
    cnt = jnp.concatenate(cnts, axis=0)
    sel_t = jnp.where((cnt < float(SLC_TOPN)) & jnp.logical_not(future), 1.0, 0.0)
    sel_ref[0, 0] = sel_t.T.astype(sel_ref.dtype)


def nsa_compressed(q, kvc, ovt, B, S, *, tq=128):
    T = B * S
    nq = S // tq
    ncp = S // CMP_STRIDE
    ns = S // SLC_LEN
    gw = NSA_G * NSA_DIM
    return pl.pallas_call(
        functools.partial(_nsa_cmp_kernel, tq=tq, scale=NSA_DIM ** -0.5),
        grid=(B, NSA_KV, nq),
        in_specs=[pl.BlockSpec((tq, gw), lambda b, h, i: (b * nq + i, h)),
                  pl.BlockSpec((1, 1, ncp, NSA_DIM), lambda b, h, i: (b, h, 0, 0)),
                  pl.BlockSpec((1, 1, ncp, NSA_DIM), lambda b, h, i: (b, NSA_KV + h, 0, 0)),
                  pl.BlockSpec((ns, ncp), lambda b, h, i: (0, 0))],
        out_specs=[pl.BlockSpec((tq, gw), lambda b, h, i: (b * nq + i, h)),
                   pl.BlockSpec((1, 1, tq, ns), lambda b, h, i: (b, h, i, 0))],
        out_shape=[jax.ShapeDtypeStruct((T, NSA_HEADS * NSA_DIM), F32),
                   jax.ShapeDtypeStruct((B, NSA_KV, S, ns), BF16)],
        compiler_params=_cparams(("arbitrary", "arbitrary", "arbitrary")),
        name="nsa_compressed",
    )(q, kvc, kvc, ovt)


def _nsa_sel_kernel(q_ref, k_ref, v_ref, sel_ref, o_ref, *, tq, tk, scale):
    i = pl.program_id(2)
    q0 = i * tq
    rows = NSA_G * tq
    ns = sel_ref.shape[3]
    qq = _stack_heads(q_ref, tq)
    sel = sel_ref[0, 0]

    blk_off = (lax.broadcasted_iota(jnp.int32, (ns, tk), 0)
               - lax.broadcasted_iota(jnp.int32, (ns, tk), 1) // SLC_LEN)
    key_off = lax.broadcasted_iota(jnp.int32, (tq, tk), 1) - lax.broadcasted_iota(jnp.int32, (tq, tk), 0)

    exp2_scale = scale * math.log2(math.e)

    def step(c, carry):
        m, l, acc = carry
        k0 = pl.multiple_of(c * tk, tk)
        kc = k_ref[pl.ds(k0, tk), :]
        vc = v_ref[pl.ds(k0, tk), :]
        s = lax.dot_general(qq, kc, _NT, preferred_element_type=F32)
        expand = jnp.where(blk_off == c * (tk // SLC_LEN), 1.0, 0.0).astype(BF16)
        picked = jnp.dot(sel, expand, preferred_element_type=F32)
        valid = (picked > 0.5) & (key_off <= q0 - k0)
        s3 = jnp.where(valid[None], s.reshape(NSA_G, tq, tk), NEG_INF)
        s = s3.reshape(rows, tk)
        m_new = jnp.maximum(m, jnp.max(s, axis=-1, keepdims=True))
        m_safe = jnp.where(m_new == NEG_INF, 0.0, m_new)
        alpha = jnp.exp2((m - m_safe) * exp2_scale)
        p = jnp.exp2((s - m_safe) * exp2_scale)
        l = alpha * l + jnp.sum(p, axis=-1, keepdims=True)
        acc = alpha * acc + jnp.dot(p.astype(BF16), vc, preferred_element_type=F32)
        return m_new, l, acc

    init = (jnp.full((rows, 1), NEG_INF, F32), jnp.zeros((rows, 1), F32), jnp.zeros((rows, LANES), F32))
    nchunk = (q0 + tq + tk - 1) // tk
    m, l, acc = lax.fori_loop(0, nchunk, step, init)
    o = acc / jnp.maximum(l, 1e-30)
    for g in range(NSA_G):
        o_ref[:, g * LANES:(g + 1) * LANES] = o[g * tq:(g + 1) * tq]


def nsa_selected(q, k, v, sel, B, S, *, k_col0, v_col0, tq=128, tk=1024):
    T = B * S
    tk = min(tk, S)
    assert S % tk == 0 and tk % SLC_LEN == 0
    nq = S // tq
    ns = S // SLC_LEN
    gw = NSA_G * NSA_DIM
    kb, vb = k_col0 // LANES, v_col0 // LANES
    return pl.pallas_call(
        functools.partial(_nsa_sel_kernel, tq=tq, tk=tk, scale=NSA_DIM ** -0.5),
        grid=(B, NSA_KV, nq),
        in_specs=[pl.BlockSpec((tq, gw), lambda b, h, i: (b * nq + i, h)),
                  pl.BlockSpec((S, LANES), lambda b, h, i: (b, kb + h)),
                  pl.BlockSpec((S, LANES), lambda b, h, i: (b, vb + h)),
                  pl.BlockSpec((1, 1, tq, ns), lambda b, h, i: (b, h, i, 0))],
        out_specs=pl.BlockSpec((tq, gw), lambda b, h, i: (b * nq + i, h)),
        out_shape=jax.ShapeDtypeStruct((T, NSA_HEADS * NSA_DIM), F32),
        compiler_params=_cparams(("arbitrary", "arbitrary", "arbitrary")),
        name="nsa_selected",
    )(q, k, v, sel)


def _nsa_win_kernel(q_ref, k_ref, v_ref, o_ref, *, tq, window, scale):
    i = pl.program_id(2)
    q0 = i * tq
    rows = NSA_G * tq
    span = tq + window
    ks = pl.multiple_of(jnp.maximum(q0 - window, 0), tq)
    qq = _stack_heads(q_ref, tq)
    kc = k_ref[pl.ds(ks, span), :]
    vc = v_ref[pl.ds(ks, span), :]
    s = lax.dot_general(qq, kc, _NT, preferred_element_type=F32)
    r = lax.broadcasted_iota(jnp.int32, (rows, span), 0) % tq
    cc = lax.broadcasted_iota(jnp.int32, (rows, span), 1)
    rel = (q0 - ks) + r - cc
    s = jnp.where((rel >= 0) & (rel < window), s, NEG_INF)
    m = jnp.max(s, axis=-1, keepdims=True)
    p = jnp.exp2((s - m) * (scale * math.log2(math.e)))
    o = jnp.dot(p.astype(BF16), vc, preferred_element_type=F32) / jnp.sum(p, axis=-1, keepdims=True)
    for g in range(NSA_G):
        o_ref[:, g * LANES:(g + 1) * LANES] = o[g * tq:(g + 1) * tq]


def nsa_window(q, k, v, B, S, *, k_col0, v_col0, tq=128):
    T = B * S
    nq = S // tq
    gw = NSA_G * NSA_DIM
    kb, vb = k_col0 // LANES, v_col0 // LANES
    assert S >= tq + NSA_WINDOW
    return pl.pallas_call(
        functools.partial(_nsa_win_kernel, tq=tq, window=NSA_WINDOW, scale=NSA_DIM ** -0.5),
        grid=(B, NSA_KV, nq),
        in_specs=[pl.BlockSpec((tq, gw), lambda b, h, i: (b * nq + i, h)),
                  pl.BlockSpec((S, LANES), lambda b, h, i: (b, kb + h)),
                  pl.BlockSpec((S, LANES), lambda b, h, i: (b, vb + h))],
        out_specs=pl.BlockSpec((tq, gw), lambda b, h, i: (b * nq + i, h)),
        out_shape=jax.ShapeDtypeStruct((T, NSA_HEADS * NSA_DIM), F32),
        compiler_params=_cparams(("arbitrary", "arbitrary", "arbitrary")),
        name="nsa_window",
    )(q, k, v)


def _nsa_mix_kernel(g_ref, oc_ref, os_ref, ow_ref, o_ref):
    sg = jax.nn.sigmoid(g_ref[...])
    for hq in range(NSA_HEADS):
        sl = slice(hq * LANES, (hq + 1) * LANES)
        acc = sg[:, hq:hq + 1] * oc_ref[:, sl]
        acc = acc + sg[:, NSA_HEADS + hq:NSA_HEADS + hq + 1] * os_ref[:, sl]
        acc = acc + sg[:, 2 * NSA_HEADS + hq:2 * NSA_HEADS + hq + 1] * ow_ref[:, sl]
        o_ref[:, sl] = acc.astype(o_ref.dtype)


def nsa_mix(gates, oc, os_, ow, *, tm=256):
    T, N = oc.shape
    spec = pl.BlockSpec((tm, N), lambda i: (i, 0))
    return pl.pallas_call(
        _nsa_mix_kernel,
        grid=(T // tm,),
        in_specs=[pl.BlockSpec((tm, LANES), lambda i: (i, 0)), spec, spec, spec],
        out_specs=spec,
        out_shape=jax.ShapeDtypeStruct((T, N), BF16),
        compiler_params=_cparams(("arbitrary",)),
        name="nsa_mix",
    )(gates, oc, os_, ow)


NO_RANK = 127.0
GELU_C0 = math.sqrt(2.0 / math.pi)
GELU_C1 = 0.044715 * GELU_C0


def _extract_sorted(s, out_ref, idx_ref, n):
    rowi = lax.broadcasted_iota(jnp.int32, s.shape, 0).astype(F32)

    def body(a, s):
        m = jnp.max(s, axis=0, keepdims=True)
        idx = jnp.min(jnp.where(s == m, rowi, 1e9), axis=0, keepdims=True)
        out_ref[pl.ds(a, 1), :] = m
        idx_ref[pl.ds(a, 1), :] = idx
        return jnp.where(rowi == idx, NEG_INF, s)

    return lax.fori_loop(0, n, body, s)


def _peer_route_kernel(q_ref, sk_ref, n0_ref, e0_ref, r1_ref, e1_ref, sv01, ix01, best, ixc):
    K = PEER_TOPK
    q = q_ref[...]
    tt = q.shape[0]
    s0 = lax.dot_general(sk_ref[0, 0], q[:, :PEER_HALF], _NT, preferred_element_type=F32)
    s1 = lax.dot_general(sk_ref[0, 1], q[:, PEER_HALF:], _NT, preferred_element_type=F32)
    _extract_sorted(jnp.concatenate([s0, s1], axis=1), sv01, ix01, K)
    a0 = sv01[:, :tt]
    a1 = sv01[:, tt:]
    cand = jnp.concatenate([a0[0:1] + a1] + [a0[a:a + 1] + a1[0:8] for a in range(1, 8)]
                           + [a0[8:16] + a1[0:1]], axis=0)
    left = _extract_sorted(cand, best, ixc, K)
    taken = jnp.where(left == NEG_INF, 1.0, 0.0)
    bv = best[...]
    z = jnp.sum(jnp.exp(bv - bv[0:1]), axis=0, keepdims=True)
    tail = taken[72:80]
    cnts = ([jnp.sum(taken[0:16], axis=0, keepdims=True)]
            + [jnp.sum(taken[8 + 8 * a:16 + 8 * a], axis=0, keepdims=True) for a in range(1, 8)]
            + [tail[a - 8:a - 7] for a in range(8, 16)])
    rowi = lax.broadcasted_iota(jnp.int32, s0.shape, 0).astype(F32)
    n0 = jnp.zeros_like(s0)
    r1 = jnp.full(s1.shape, NO_RANK, F32)
    for a in range(K):
        n0 = jnp.where(rowi == ix01[a:a + 1, :tt], cnts[a], n0)
        r1 = jnp.where(rowi == ix01[a:a + 1, tt:], float(a), r1)
    n0_ref[0] = n0
    e0_ref[0] = jnp.exp(s0 - a0[0:1])
    r1_ref[0] = r1
    e1_ref[0] = 0.5 * jnp.exp(s1 - a1[0:1]) / z


def peer_route(q, subkeys, *, tt=512):
    T = q.shape[0]
    H = PEER_HEADS
    f32 = jax.ShapeDtypeStruct((H, PEER_NKEYS, T), F32)
    bspec = pl.BlockSpec((1, PEER_NKEYS, tt), lambda i, h: (h, 0, i))
    return pl.pallas_call(
        _peer_route_kernel,
        grid=(T // tt, H),
        in_specs=[pl.BlockSpec((tt, 2 * PEER_HALF), lambda i, h: (i, h)),
                  pl.BlockSpec((1, 2, PEER_NKEYS, PEER_HALF), lambda i, h: (h, 0, 0, 0))],
        out_specs=[bspec, bspec, bspec, bspec],
        out_shape=[f32, f32, f32, f32],
        scratch_shapes=[pltpu.VMEM((PEER_TOPK, 2 * tt), F32), pltpu.VMEM((PEER_TOPK, 2 * tt), F32),
                        pltpu.VMEM((PEER_TOPK, tt), F32), pltpu.VMEM((PEER_TOPK, tt), F32)],
        compiler_params=_cparams(("arbitrary", "arbitrary")),
        name="peer_route",
    )(q, subkeys.astype(BF16))


def _peer_mix_kernel(ht_ref, u_ref, vt_ref, n0_ref, e0_ref, r1_ref, e1_ref, x_ref, gate_ref, *rest, te, final_norm):
    if final_norm:
        fg_ref, o_ref, acc_ref, at0, at1, cf0, cf1 = rest
    else:
        o_ref, acc_ref, at0, at1, cf0, cf1 = rest
    g = pl.program_id(1)
    tt = ht_ref.shape[1]
    nslab = te // PEER_NKEYS

    @pl.when(g == 0)
    def _():
        acc_ref[...] = jnp.zeros_like(acc_ref)

    tile_rows = 64

    def build(at_ref, cf_ref, row0):
        for ii in range(nslab):
            n0rows = [n0_ref[hh, pl.ds(row0 + ii, 1), :] for hh in range(PEER_HEADS)]
            e0rows = [e0_ref[hh, pl.ds(row0 + ii, 1), :] for hh in range(PEER_HEADS)]
            for ts in range(tt // LANES):
                lanes = slice(ts * LANES, (ts + 1) * LANES)
                for part in range(PEER_NKEYS // tile_rows):
                    jr = slice(part * tile_rows, (part + 1) * tile_rows)
                    er = slice(ii * PEER_NKEYS + part * tile_rows, ii * PEER_NKEYS + (part + 1) * tile_rows)
                    w = None
                    for hh in range(PEER_HEADS):
                        term = jnp.where(r1_ref[hh, jr, lanes] < n0rows[hh][:, lanes],
                                         e0rows[hh][:, lanes] * e1_ref[hh, jr, lanes], 0.0)
                        w = term if w is None else w + term
                    a = at_ref[er, lanes]
                    act = a + a * jnp.tanh(a * (GELU_C0 + GELU_C1 * (a * a)))
                    cf_ref[er, lanes] = (w * act).astype(BF16)

    for t, (at_ref, cf_ref) in enumerate(((at0, cf0), (at1, cf1))):
        er = slice(t * te, (t + 1) * te)
        at_ref[...] = jnp.dot(u_ref[er, :], ht_ref[...], preferred_element_type=F32)
        build(at_ref, cf_ref, (2 * g + t) * nslab)
        acc_ref[...] += jnp.dot(vt_ref[:, er], cf_ref[...], preferred_element_type=F32)

    @pl.when(g == pl.num_programs(1) - 1)
    def _():
        y = x_ref[...] + gate_ref[0] * acc_ref[...].T
        if final_norm:
            y = y * lax.rsqrt(jnp.mean(y * y, axis=-1, keepdims=True) + NORM_EPS) * fg_ref[...]
        o_ref[...] = y


def peer_mix(ht, u, vt, route, x2, gate, S, *, final_g=None, tt=512, te=512):
    D, T = ht.shape
    E = u.shape[0]
    H = PEER_HEADS
    per = S // tt
    B = T // S
    n0, e0, r1, e1 = route
    once = pl.Buffered(1)
    bspec = pl.BlockSpec((H, PEER_NKEYS, tt), lambda i, g: (0, 0, i), pipeline_mode=once)
    in_specs = [pl.BlockSpec((D, tt), lambda i, g: (0, i)),
                pl.BlockSpec((2 * te, D), lambda i, g: (g, 0)),
                pl.BlockSpec((D, 2 * te), lambda i, g: (0, g)),
                bspec, bspec, bspec, bspec,
                pl.BlockSpec((tt, D), lambda i, g: (i, 0), pipeline_mode=once),
                pl.BlockSpec((1, 1, D), lambda i, g: (i // per, 0, 0))]
    args = [ht, u, vt, n0, e0, r1, e1, x2, gate.reshape(B, 1, D)]
    if final_g is not None:
        in_specs.append(pl.BlockSpec((1, D), lambda i, g: (0, 0)))
        args.append(final_g.reshape(1, D))
    return pl.pallas_call(
        functools.partial(_peer_mix_kernel, te=te, final_norm=final_g is not None),
        grid=(T // tt, E // (2 * te)),
        in_specs=in_specs,
        out_specs=pl.BlockSpec((tt, D), lambda i, g: (i, 0)),
        out_shape=jax.ShapeDtypeStruct((T, D), F32),
        scratch_shapes=[pltpu.VMEM((D, tt), F32), pltpu.VMEM((te, tt), F32), pltpu.VMEM((te, tt), F32),
                        pltpu.VMEM((te, tt), BF16), pltpu.VMEM((te, tt), BF16)],
        compiler_params=_cparams(("arbitrary", "arbitrary"), vmem_mb=56),
        name="peer_mix",
    )(*args)


def _even_w_in_layout(w):
    D = w.shape[0]
    da = DA_HEADS * DA_DIM
    qa = w[:, 0:2 * da].reshape(D, 2, DA_HEADS, DA_DIM)
    ka = w[:, 2 * da:4 * da].reshape(D, 2, DA_HEADS, DA_DIM)
    o = 4 * da
    va = w[:, o:o + DA_HEADS * DA_VDIM]
    o += DA_HEADS * DA_VDIM
    qs = w[:, o:o + SW_HEADS * SW_DIM]
    o += SW_HEADS * SW_DIM
    ks = w[:, o:o + SW_KV * SW_DIM].reshape(D, SW_KV, 1, SW_DIM)
    o += SW_KV * SW_DIM
    vs = w[:, o:o + SW_KV * SW_DIM].reshape(D, SW_KV, 1, SW_DIM)
    scale = DA_DIM ** -0.5
    assert scale == 0.125 and SW_DIM == DA_DIM
    qa2 = (qa * scale).transpose(0, 2, 1, 3).reshape(D, 2 * da)
    ka2 = ka.transpose(0, 2, 1, 3).reshape(D, 2 * da)
    ks2 = jnp.broadcast_to(ks, (D, SW_KV, 2, SW_DIM)).reshape(D, 2 * SW_KV * SW_DIM)
    vs2 = jnp.broadcast_to(vs, (D, SW_KV, 2, SW_DIM)).reshape(D, 2 * SW_KV * SW_DIM)
    return jnp.concatenate([qa2, ka2, qs * scale, ks2, va, vs2], axis=1).astype(BF16)


def _odd_w_in_layout(w):
    D = w.shape[0]
    kvw = NSA_KV * NSA_DIM
    o = NSA_HEADS * NSA_DIM
    q = w[:, :o]
    kc, vc, ksl, vsl, kw, vw = [w[:, o + n * kvw:o + (n + 1) * kvw] for n in range(6)]
    gates = w[:, o + 6 * kvw:]
    pad = jnp.zeros((D, 3840 - (o + 6 * kvw + gates.shape[1])), w.dtype)
    return jnp.concatenate([q, ksl, kw, vsl, vw, kc, vc, gates, pad], axis=1).astype(BF16)


def _overlap_t(S):
    ncp = S // CMP_STRIDE
    ns = S // SLC_LEN
    cst = np.arange(ncp)[None, :] * CMP_STRIDE
    sst = np.arange(ns)[:, None] * SLC_LEN
    ov = np.clip(np.minimum(cst + CMP_LEN, sst + SLC_LEN) - np.maximum(cst, sst), 0, None)
    return jnp.asarray(ov.astype(np.float32) / CMP_LEN, dtype=BF16)


def _peer_layer(x2, mod, g, wq, subkeys, u_tab, v_tab, B, S, final_g=None):
    D = x2.shape[1]
    shift, scale, gate = mod[:, :D], mod[:, D:2 * D], mod[:, 2 * D:]
    q, h = norm_mod_matmul(x2, g, scale, shift, wq.astype(BF16), S, emit_h=True, out_dtype=BF16)
    route = peer_route(q, subkeys)
    return peer_mix(h.T, u_tab.astype(BF16), v_tab.astype(BF16).T, route, x2, gate, S, final_g=final_g)


def _even_layer(x2, mod, g, w_in, w_out, lam_vecs, subln_g, sinks, lam_init, cos64, sin64, B, S):
    D = x2.shape[1]
    shift, scale, gate = mod[:, :D], mod[:, D:2 * D], mod[:, 2 * D:]
    y = norm_mod_matmul(x2, g, scale, shift, _even_w_in_layout(w_in), S)
    qk = rope_cols(y, 0, 3584, cos64, sin64, 64, S)
    v = cast_cols(y, 3584, 1536, BF16, S)
    oa = diff_attention(qk, v, lam_vecs, subln_g, B, S, lam_init, q_col0=0, k_col0=1024, v_col0=0)
    ob = swa_attention(qk, v, sinks, B, S, q_col0=2048, k_col0=3072, v_col0=1024)
    o = jnp.concatenate([oa, ob], axis=1)
    return matmul_residual(o, w_out.astype(BF16), x2, gate, S)


def _odd_layer(x2, mod, g, w_in, w_out, cmp_pos, cmp_w1, cmp_w2, cos128, sin128, cos_c, sin_c, B, S):
    D = x2.shape[1]
    shift, scale, gate = mod[:, :D], mod[:, D:2 * D], mod[:, 2 * D:]
    y = norm_mod_matmul(x2, g, scale, shift, _odd_w_in_layout(w_in), S, tn=1280)
    qk = rope_cols(y, 0, 2560, cos128, sin128, 128, S)
    v = cast_cols(y, 2560, 512, BF16, S)
    xc = cast_cols(y, 3072, 512, F32, S)
    gates = cast_cols(y, 3584, 128, F32, S, cw=128)
    kvc = nsa_compress(xc, cmp_pos, cmp_w1, cmp_w2, cos_c, sin_c, B, S)
    oc, sel = nsa_compressed(qk, kvc, _overlap_t(S), B, S)
    os_ = nsa_selected(qk, qk, v, sel, B, S, k_col0=2048, v_col0=0)
    ow = nsa_window(qk, qk, v, B, S, k_col0=2304, v_col0=256)
    o = nsa_mix(gates, oc, os_, ow)
    return matmul_residual(o, w_out.astype(BF16), x2, gate, S)


def kernel(x, c, ada_w, ada_b, norm_g, even_w_in, even_w_out, da_lambda, da_subln, sw_sinks, odd_w_in, odd_w_out, nsa_cmp_pos, nsa_cmp_w1, nsa_cmp_w2, peer_wq, peer_subkeys, peer_u, peer_v, final_g):
    B, S, D = x.shape
    depth = ada_w.shape[0]
    pos = jnp.arange(S)
    cos64, sin64 = _rope_tables(pos, DA_DIM)
    cos128, sin128 = _rope_tables(pos, NSA_DIM)
    cos_c, sin_c = _rope_tables(jnp.arange(S // CMP_STRIDE) * CMP_STRIDE + CMP_LEN - 1, NSA_DIM)
    mods = ada_mod(c, ada_w, ada_b)
    x2 = x.reshape(B * S, D)
    for layer in range(depth):
        j = layer // 2
        if layer % 2 == 0:
            lam_init = 0.8 - 0.6 * math.exp(-0.3 * layer)
            x2 = _even_layer(x2, mods[2 * layer], norm_g[layer, 0], even_w_in[j], even_w_out[j],
                             da_lambda[j], da_subln[j], sw_sinks[j], lam_init, cos64, sin64, B, S)
        else:
            x2 = _odd_layer(x2, mods[2 * layer], norm_g[layer, 0], odd_w_in[j], odd_w_out[j],
                            nsa_cmp_pos[j], nsa_cmp_w1[j], nsa_cmp_w2[j], cos128, sin128, cos_c, sin_c, B, S)
        x2 = _peer_layer(x2, mods[2 * layer + 1], norm_g[layer, 1], peer_wq[layer], peer_subkeys[layer],
                         peer_u[layer], peer_v[layer], B, S,
                         final_g=final_g if layer == depth - 1 else None)
    return x2.reshape(B, S, D)
```

```python
import functools
import math

import jax
import jax.numpy as jnp
import numpy as np
from jax import lax
from jax.experimental import pallas as pl
from jax.experimental.pallas import tpu as pltpu

F32 = jnp.float32
BF16 = jnp.bfloat16
NEG_INF = float("-inf")

ROPE_THETA = 10000.0
NORM_EPS = 1e-6

DA_HEADS = 8
DA_DIM = 64
DA_VDIM = 128
SW_HEADS = 16
SW_KV = 4
SW_DIM = 64
SW_WINDOW = 128

NSA_HEADS = 16
NSA_KV = 2
NSA_G = NSA_HEADS // NSA_KV
NSA_DIM = 128
CMP_LEN = 32
CMP_STRIDE = 16
CMP_HIDDEN = 256
SLC_LEN = 64
SLC_TOPN = 16
NSA_WINDOW = 512

PEER_HEADS = 8
PEER_NKEYS = 128
PEER_TOPK = 16
PEER_HALF = 128

LANES = 128
V7X_VMEM_BYTES = 64 * 1024 * 1024

_NT = (((1,), (1,)), ((), ()))


def _cparams(sem, vmem_mb=48):
    assert vmem_mb * 1024 * 1024 < V7X_VMEM_BYTES
    return pltpu.CompilerParams(dimension_semantics=sem, vmem_limit_bytes=vmem_mb * 1024 * 1024)


def _ada_kernel(c_ref, w_ref, b_ref, o_ref):
    c = c_ref[...]
    sc = c * jax.nn.sigmoid(c)
    o_ref[0] = jnp.dot(sc, w_ref[0], preferred_element_type=F32,
                       precision=lax.Precision.HIGHEST) + b_ref[0]


def ada_mod(c, ada_w, ada_b):
    B, D = c.shape
    n = ada_w.shape[0] * ada_w.shape[1]
    w = ada_w.reshape(n, D, 3 * D)
    b = ada_b.reshape(n, 1, 3 * D)
    rows = 8
    cp = jnp.zeros((rows, D), F32).at[:B].set(c)
    tn = 512
    out = pl.pallas_call(
        _ada_kernel,
        grid=(n, 3 * D // tn),
        in_specs=[pl.BlockSpec((rows, D), lambda l, j: (0, 0)),
                  pl.BlockSpec((1, D, tn), lambda l, j: (l, 0, j)),
                  pl.BlockSpec((1, 1, tn), lambda l, j: (l, 0, j))],
        out_specs=pl.BlockSpec((1, rows, tn), lambda l, j: (l, 0, j)),
        out_shape=jax.ShapeDtypeStruct((n, rows, 3 * D), F32),
        compiler_params=_cparams(("arbitrary", "arbitrary")),
        name="ada_mod",
    )(cp, w, b)
    return out[:, :B]


def _nmm_kernel(x_ref, g_ref, sc_ref, sh_ref, w_ref, *rest, emit_h):
    if emit_h:
        y_ref, h_out_ref, hs_ref = rest
    else:
        y_ref, hs_ref = rest

    @pl.when(pl.program_id(1) == 0)
    def _():
        x = x_ref[...]
        ms = jnp.mean(x * x, axis=-1, keepdims=True)
        y = x * lax.rsqrt(ms + NORM_EPS) * g_ref[...]
        h = (y * (1.0 + sc_ref[0]) + sh_ref[0]).astype(BF16)
        hs_ref[...] = h
        if emit_h:
            h_out_ref[...] = h

    y_ref[...] = jnp.dot(hs_ref[...], w_ref[...], preferred_element_type=F32).astype(y_ref.dtype)


def norm_mod_matmul(x2, g, scale, shift, w, S, *, emit_h=False, out_dtype=F32, tm=1024, tn=1024):
    T, D = x2.shape
    N = w.shape[1]
    tm = min(tm, S)
    assert S % tm == 0 and N % tn == 0
    per = S // tm
    B = T // S
    out_shape = [jax.ShapeDtypeStruct((T, N), out_dtype)]
    out_specs = [pl.BlockSpec((tm, tn), lambda i, j: (i, j))]
    if emit_h:
        out_shape.append(jax.ShapeDtypeStruct((T, D), BF16))
        out_specs.append(pl.BlockSpec((tm, D), lambda i, j: (i, 0)))
    res = pl.pallas_call(
        functools.partial(_nmm_kernel, emit_h=emit_h),
        grid=(T // tm, N // tn),
        in_specs=[pl.BlockSpec((tm, D), lambda i, j: (i, 0)),
                  pl.BlockSpec((1, D), lambda i, j: (0, 0)),
                  pl.BlockSpec((1, 1, D), lambda i, j: (i // per, 0, 0)),
                  pl.BlockSpec((1, 1, D), lambda i, j: (i // per, 0, 0)),
                  pl.BlockSpec((D, tn), lambda i, j: (0, j))],
        out_specs=out_specs,
        out_shape=out_shape,
        scratch_shapes=[pltpu.VMEM((tm, D), BF16)],
        compiler_params=_cparams(("arbitrary", "arbitrary"), vmem_mb=56),
        name="norm_mod_matmul",
    )(x2, g.reshape(1, D), scale.reshape(B, 1, D), shift.reshape(B, 1, D), w)
    return res if emit_h else res[0]


def _mmres_kernel(a_ref, w_ref, x_ref, gate_ref, o_ref):
    y = jnp.dot(a_ref[...], w_ref[...], preferred_element_type=F32)
    o_ref[...] = x_ref[...] + gate_ref[0] * y


def matmul_residual(a, w, x2, gate, S, *, tm=1024, tn=1024):
    T, K = a.shape
    N = w.shape[1]
    tm = min(tm, S)
    per = S // tm
    B = T // S
    return pl.pallas_call(
        _mmres_kernel,
        grid=(T // tm, N // tn),
        in_specs=[pl.BlockSpec((tm, K), lambda i, j: (i, 0)),
                  pl.BlockSpec((K, tn), lambda i, j: (0, j)),
                  pl.BlockSpec((tm, tn), lambda i, j: (i, j)),
                  pl.BlockSpec((1, 1, tn), lambda i, j: (i // per, 0, j))],
        out_specs=pl.BlockSpec((tm, tn), lambda i, j: (i, j)),
        out_shape=jax.ShapeDtypeStruct((T, N), F32),
        compiler_params=_cparams(("arbitrary", "arbitrary")),
        name="matmul_residual",
    )(a, w, x2, gate.reshape(B, 1, N))


def _rope_block(v, cos, sin, hd):
    if hd == 64:
        lane = lax.broadcasted_iota(jnp.int32, v.shape, 1)
        lo = (lane % 64) < 32
        partner = jnp.where(lo, pltpu.roll(v, 96, 1), pltpu.roll(v, 32, 1))
    else:
        partner = pltpu.roll(v, 64, 1)
    return v * cos + partner * sin


def _rope_kernel(y_ref, cos_ref, sin_ref, o_ref, *, hd):
    cos = cos_ref[...]
    sin = sin_ref[...]
    for k in range(o_ref.shape[1] // LANES):
        sl = slice(k * LANES, (k + 1) * LANES)
        o_ref[:, sl] = _rope_block(y_ref[:, sl], cos, sin, hd).astype(o_ref.dtype)


def _cast_kernel(y_ref, o_ref):
    o_ref[...] = y_ref[...].astype(o_ref.dtype)


def rope_cols(y, col0, ncols, cos, sin, hd, S, *, tm=1024, cw=512):
    T = y.shape[0]
    tm = min(tm, S)
    per = S // tm
    assert col0 % cw == 0 and ncols % cw == 0
    c0 = col0 // cw
    return pl.pallas_call(
        functools.partial(_rope_kernel, hd=hd),
        grid=(T // tm, ncols // cw),
        in_specs=[pl.BlockSpec((tm, cw), lambda i, j: (i, c0 + j)),
                  pl.BlockSpec((tm, LANES), lambda i, j: (i % per, 0)),
                  pl.BlockSpec((tm, LANES), lambda i, j: (i % per, 0))],
        out_specs=pl.BlockSpec((tm, cw), lambda i, j: (i, j)),
        out_shape=jax.ShapeDtypeStruct((T, ncols), BF16),
        compiler_params=_cparams(("arbitrary", "arbitrary")),
        name="rope_cols",
    )(y, cos, sin)


def cast_cols(y, col0, ncols, dtype, S, *, tm=1024, cw=512):
    T = y.shape[0]
    tm = min(tm, S)
    cw = min(cw, ncols)
    assert col0 % cw == 0 and ncols % cw == 0
    c0 = col0 // cw
    return pl.pallas_call(
        _cast_kernel,
        grid=(T // tm, ncols // cw),
        in_specs=[pl.BlockSpec((tm, cw), lambda i, j: (i, c0 + j))],
        out_specs=pl.BlockSpec((tm, cw), lambda i, j: (i, j)),
        out_shape=jax.ShapeDtypeStruct((T, ncols), dtype),
        compiler_params=_cparams(("arbitrary", "arbitrary")),
        name="cast_cols",
    )(y)


def _rope_tables(pos, hd):
    inv = jnp.power(ROPE_THETA, -jnp.arange(0, hd, 2, dtype=F32) / hd)
    ang = pos.astype(F32)[:, None] * inv[None, :]
    cos, sin = jnp.cos(ang), jnp.sin(ang)
    reps = LANES // hd
    cos_l = jnp.tile(jnp.concatenate([cos, cos], axis=1), (1, reps))
    sin_l = jnp.tile(jnp.concatenate([-sin, sin], axis=1), (1, reps))
    return cos_l, sin_l


def _split_halves(q):
    lane = lax.broadcasted_iota(jnp.int32, q.shape, 1)
    zero = jnp.zeros_like(q)
    return jnp.concatenate([jnp.where(lane < 64, q, zero), jnp.where(lane >= 64, q, zero)], axis=0)


def _diff_kernel(q_ref, k_ref, v_ref, lv_ref, sg_ref, o_ref, *, tq, tk, lam_init):
    i = pl.program_id(2)
    q0 = i * tq
    qq = _split_halves(q_ref[...])
    rows = 2 * tq

    def step(c, carry, masked):
        m, l, acc = carry
        k0 = pl.multiple_of(c * tk, tk)
        kc = k_ref[pl.ds(k0, tk), :]
        vc = v_ref[pl.ds(k0, tk), :]
        s = lax.dot_general(qq, kc, _NT, preferred_element_type=F32)
        if masked:
            r = lax.broadcasted_iota(jnp.int32, (rows, tk), 0) % tq
            cc = lax.broadcasted_iota(jnp.int32, (rows, tk), 1)
            s = jnp.where(cc - r <= q0 - k0, s, NEG_INF)
        m_new = jnp.maximum(m, jnp.max(s, axis=-1, keepdims=True))
        alpha = jnp.exp(m - m_new)
        p = jnp.exp(s - m_new)
        l = alpha * l + jnp.sum(p, axis=-1, keepdims=True)
        acc = alpha * acc + jnp.dot(p.astype(BF16), vc, preferred_element_type=F32)
        return m_new, l, acc

    init = (jnp.full((rows, 1), NEG_INF, F32), jnp.zeros((rows, 1), F32), jnp.zeros((rows, LANES), F32))
    nfull = q0 // tk
    carry = lax.fori_loop(0, nfull, functools.partial(step, masked=False), init)
    m, l, acc = step(nfull, carry, True)
    o = acc / l
    lv = lv_ref[...]
    lam = (jnp.exp(jnp.sum(lv[0:1] * lv[1:2], axis=-1, keepdims=True))
           - jnp.exp(jnp.sum(lv[2:3] * lv[3:4], axis=-1, keepdims=True)) + lam_init)
    o = o[:tq] - lam * o[tq:]
    o = o * lax.rsqrt(jnp.mean(o * o, axis=-1, keepdims=True) + NORM_EPS) * sg_ref[...]
    o_ref[...] = (o * (1.0 - lam_init)).astype(o_ref.dtype)


def diff_attention(qk, v, lam_vecs, subln_g, B, S, lam_init, *, q_col0, k_col0, v_col0, tq=512, tk=1024):
    T = B * S
    tq = min(tq, S)
    tk = min(tk, S)
    assert tk % tq == 0 and S % tk == 0
    nq = S // tq
    H = DA_HEADS
    qb, kb, vb = q_col0 // LANES, k_col0 // LANES, v_col0 // LANES
    return pl.pallas_call(
        functools.partial(_diff_kernel, tq=tq, tk=tk, lam_init=lam_init),
        grid=(B, H, nq),
        in_specs=[pl.BlockSpec((tq, LANES), lambda b, h, i: (b * nq + i, qb + h)),
                  pl.BlockSpec((S, LANES), lambda b, h, i: (b, kb + h)),
                  pl.BlockSpec((S, LANES), lambda b, h, i: (b, vb + h)),
                  pl.BlockSpec((4, DA_DIM), lambda b, h, i: (0, 0)),
                  pl.BlockSpec((1, DA_VDIM), lambda b, h, i: (0, 0))],
        out_specs=pl.BlockSpec((tq, LANES), lambda b, h, i: (b * nq + i, h)),
        out_shape=jax.ShapeDtypeStruct((T, H * DA_VDIM), BF16),
        compiler_params=_cparams(("arbitrary", "arbitrary", "arbitrary")),
        name="diff_attention",
    )(qk, qk, v, lam_vecs.astype(F32), subln_g.reshape(1, DA_VDIM).astype(F32))


def _swa_kernel(sink_ref, q_ref, k_ref, v_ref, o_ref, *, tq, window):
    pr = pl.program_id(1)
    i = pl.program_id(2)
    q0 = i * tq
    span = tq + window
    ks = pl.multiple_of(jnp.maximum(q0 - window, 0), window)
    qq = _split_halves(q_ref[...])
    rows = 2 * tq
    kc = k_ref[pl.ds(ks, span), :]
    vc = v_ref[pl.ds(ks, span), :]
    s = lax.dot_general(qq, kc, _NT, preferred_element_type=F32)
    r = lax.broadcasted_iota(jnp.int32, (rows, span), 0) % tq
    cc = lax.broadcasted_iota(jnp.int32, (rows, span), 1)
    rel = (q0 - ks) + r - cc
    s = jnp.where((rel >= 0) & (rel < window), s, NEG_INF)
    rr = lax.broadcasted_iota(jnp.int32, (rows, 1), 0)
    sink = jnp.where(rr < tq, sink_ref[2 * pr], sink_ref[2 * pr + 1])
    m = jnp.maximum(jnp.max(s, axis=-1, keepdims=True), sink)
    p = jnp.exp(s - m)
    den = jnp.sum(p, axis=-1, keepdims=True) + jnp.exp(sink - m)
    o2 = jnp.dot(p.astype(BF16), vc, preferred_element_type=F32) / den
    lane = lax.broadcasted_iota(jnp.int32, (tq, LANES), 1)
    o_ref[...] = jnp.where(lane < 64, o2[:tq], o2[tq:]).astype(o_ref.dtype)


def swa_attention(qk, v, sinks, B, S, *, q_col0, k_col0, v_col0, tq=512):
    T = B * S
    tq = min(tq, S)
    nq = S // tq
    npair = SW_HEADS // 2
    ppk = (SW_HEADS // SW_KV) // 2
    qb, kb, vb = q_col0 // LANES, k_col0 // LANES, v_col0 // LANES
    return pl.pallas_call(
        functools.partial(_swa_kernel, tq=tq, window=SW_WINDOW),
        grid=(B, npair, nq),
        in_specs=[pl.BlockSpec(memory_space=pltpu.SMEM),
                  pl.BlockSpec((tq, LANES), lambda b, p, i: (b * nq + i, qb + p)),
                  pl.BlockSpec((S, LANES), lambda b, p, i: (b, kb + p // ppk)),
                  pl.BlockSpec((S, LANES), lambda b, p, i: (b, vb + p // ppk))],
        out_specs=pl.BlockSpec((tq, LANES), lambda b, p, i: (b * nq + i, p)),
        out_shape=jax.ShapeDtypeStruct((T, SW_HEADS * SW_DIM), BF16),
        compiler_params=_cparams(("arbitrary", "arbitrary", "arbitrary")),
        name="swa_attention",
    )(sinks.astype(F32), qk, qk, v)


def _compress_kernel(x_ref, pos_ref, w1a_ref, w1b_ref, w2_ref, cos_ref, sin_ref, o_ref, acc_a, acc_b):
    tok = pl.program_id(1)
    ntok = pl.num_programs(1)

    @pl.when(tok == 0)
    def _():
        acc_a[...] = jnp.zeros_like(acc_a)
        acc_b[...] = jnp.zeros_like(acc_b)

    for kv in range(2):
        pa = pos_ref[kv, pl.ds(tok, 1), :]
        pb = pos_ref[kv, pl.ds(CMP_STRIDE + tok, 1), :]
        for h in range(NSA_KV):
            idx = kv * NSA_KV + h
            xs = x_ref[0, :, idx * LANES:(idx + 1) * LANES]
            acc_a[idx] += jnp.dot((xs + pa).astype(BF16), w1a_ref[kv], preferred_element_type=F32)
            acc_b[idx] += jnp.dot((xs + pb).astype(BF16), w1b_ref[kv], preferred_element_type=F32)

    @pl.when(tok == ntok - 1)
    def _():
        ncp = acc_a.shape[1]
        for kv in range(2):
            for h in range(NSA_KV):
                idx = kv * NSA_KV + h
                hid = acc_a[idx] + pltpu.roll(acc_b[idx], ncp - 1, 0)
                out = jnp.dot(jax.nn.gelu(hid).astype(BF16), w2_ref[kv], preferred_element_type=F32)
                if kv == 0:
                    out = _rope_block(out, cos_ref[...], sin_ref[...], NSA_DIM)
                o_ref[0, idx] = out.astype(o_ref.dtype)


def nsa_compress(xc, cmp_pos, cmp_w1, cmp_w2, cos_c, sin_c, B, S):
    ncp = S // CMP_STRIDE
    width = 2 * NSA_KV * NSA_DIM
    x3 = xc.reshape(B, ncp, CMP_STRIDE * width)
    half = CMP_STRIDE * NSA_DIM
    w1 = cmp_w1.astype(BF16)
    return pl.pallas_call(
        _compress_kernel,
        grid=(B, CMP_STRIDE),
        in_specs=[pl.BlockSpec((1, ncp, width), lambda b, t: (b, 0, t)),
                  pl.BlockSpec((2, CMP_LEN, NSA_DIM), lambda b, t: (0, 0, 0)),
                  pl.BlockSpec((2, NSA_DIM, CMP_HIDDEN), lambda b, t: (0, t, 0)),
                  pl.BlockSpec((2, NSA_DIM, CMP_HIDDEN), lambda b, t: (0, CMP_STRIDE + t, 0)),
                  pl.BlockSpec((2, CMP_HIDDEN, NSA_DIM), lambda b, t: (0, 0, 0)),
                  pl.BlockSpec((ncp, LANES), lambda b, t: (0, 0)),
                  pl.BlockSpec((ncp, LANES), lambda b, t: (0, 0))],
        out_specs=pl.BlockSpec((1, 2 * NSA_KV, ncp, NSA_DIM), lambda b, t: (b, 0, 0, 0)),
        out_shape=jax.ShapeDtypeStruct((B, 2 * NSA_KV, ncp, NSA_DIM), BF16),
        scratch_shapes=[pltpu.VMEM((2 * NSA_KV, ncp, CMP_HIDDEN), F32),
                        pltpu.VMEM((2 * NSA_KV, ncp, CMP_HIDDEN), F32)],
        compiler_params=_cparams(("arbitrary", "arbitrary")),
        name="nsa_compress",
    )(x3, cmp_pos.astype(F32), w1, w1, cmp_w2.astype(BF16), cos_c, sin_c)


def _stack_heads(q_ref, tq):
    return jnp.concatenate([q_ref[:, g * LANES:(g + 1) * LANES] for g in range(NSA_G)], axis=0)


def _nsa_cmp_kernel(q_ref, kc_ref, vc_ref, ovt_ref, o_ref, sel_ref, *, tq, scale):
    i = pl.program_id(2)
    q0 = i * tq
    rows = NSA_G * tq
    ncp = kc_ref.shape[2]
    ns = ovt_ref.shape[0]
    qq = _stack_heads(q_ref, tq)
    s = lax.dot_general(qq, kc_ref[0, 0], _NT, preferred_element_type=F32) * scale
    tpos = q0 + lax.broadcasted_iota(jnp.int32, (rows, ncp), 0) % tq
    cend = lax.broadcasted_iota(jnp.int32, (rows, ncp), 1) * CMP_STRIDE + (CMP_LEN - 1)
    s = jnp.where(cend <= tpos, s, NEG_INF)
    m = jnp.max(s, axis=-1, keepdims=True)
    m = jnp.where(m == NEG_INF, 0.0, m)
    p = jnp.exp(s - m)
    p = p / jnp.maximum(jnp.sum(p, axis=-1, keepdims=True), 1e-30)
    o = jnp.dot(p.astype(BF16), vc_ref[0, 0], preferred_element_type=F32)
    for g in range(NSA_G):
        o_ref[:, g * LANES:(g + 1) * LANES] = o[g * tq:(g + 1) * tq]
    psum = p[0:tq]
    for g in range(1, NSA_G):
        psum = psum + p[g * tq:(g + 1) * tq]
    p_hi = psum.astype(BF16)
    p_lo = (psum - p_hi.astype(F32)).astype(BF16)
    ovt = ovt_ref[...]
    imp = (lax.dot_general(ovt, p_hi, _NT, preferred_element_type=F32)
           + lax.dot_general(ovt, p_lo, _NT, preferred_element_type=F32))
    blk = lax.broadcasted_iota(jnp.int32, (ns, tq), 0)
    cur = (q0 + lax.broadcasted_iota(jnp.int32, (ns, tq), 1)) // SLC_LEN
    future = blk > cur
    forced = (blk == 0) | (blk == cur) | (blk == cur - 1)
    imp = jnp.where(forced, jnp.inf, imp)
    imp = jnp.where(future, NEG_INF, imp)
    sub = 8
    groups = [imp[g * sub:(g + 1) * sub] for g in range(ns // sub)]
    cnts = [jnp.zeros((sub, tq), F32) for _ in groups]
    below = lax.broadcasted_iota(jnp.int32, (sub, tq), 0)
    for sp in range(ns):
        row = imp[sp:sp + 1]
        for g, vals in enumerate(groups):
            if (g + 1) * sub - 1 < sp:
                beats = row > vals
            elif g * sub > sp:
                beats = row >= vals
            else:
                beats = (row > vals) | ((row == vals) & (below > sp - g * sub))
            cnts[g] = cnts[g] + jnp.where(beats, 1.0, 0.0)
    cnt = jnp.concatenate(cnts, axis=0)
    sel_t = jnp.where((cnt < float(SLC_TOPN)) & jnp.logical_not(future), 1.0, 0.0)
    sel_ref[0, 0] = sel_t.T.astype(sel_ref.dtype)


def nsa_compressed(q, kvc, ovt, B, S, *, tq=128):
    T = B * S
    nq = S // tq
    ncp = S // CMP_STRIDE
    ns = S // SLC_LEN
    gw = NSA_G * NSA_DIM
    return pl.pallas_call(
        functools.partial(_nsa_cmp_kernel, tq=tq, scale=NSA_DIM ** -0.5),
        grid=(B, NSA_KV, nq),
        in_specs=[pl.BlockSpec((tq, gw), lambda b, h, i: (b * nq + i, h)),
                  pl.BlockSpec((1, 1, ncp, NSA_DIM), lambda b, h, i: (b, h, 0, 0)),
                  pl.BlockSpec((1, 1, ncp, NSA_DIM), lambda b, h, i: (b, NSA_KV + h, 0, 0)),
                  pl.BlockSpec((ns, ncp), lambda b, h, i: (0, 0))],
        out_specs=[pl.BlockSpec((tq, gw), lambda b, h, i: (b * nq + i, h)),
                   pl.BlockSpec((1, 1, tq, ns), lambda b, h, i: (b, h, i, 0))],
        out_shape=[jax.ShapeDtypeStruct((T, NSA_HEADS * NSA_DIM), F32),
                   jax.ShapeDtypeStruct((B, NSA_KV, S, ns), BF16)],
        compiler_params=_cparams(("arbitrary", "arbitrary", "arbitrary")),
        name="nsa_compressed",
    )(q, kvc, kvc, ovt)


def _nsa_sel_kernel(q_ref, k_ref, v_ref, sel_ref, o_ref, *, tq, tk, scale):
    i = pl.program_id(2)
    q0 = i * tq
    rows = NSA_G * tq
    ns = sel_ref.shape[3]
    qq = _stack_heads(q_ref, tq)
    sel = sel_ref[0, 0]

    blk_off = (lax.broadcasted_iota(jnp.int32, (ns, tk), 0)
               - lax.broadcasted_iota(jnp.int32, (ns, tk), 1) // SLC_LEN)
    key_off = lax.broadcasted_iota(jnp.int32, (tq, tk), 1) - lax.broadcasted_iota(jnp.int32, (tq, tk), 0)

    exp2_scale = scale * math.log2(math.e)

    def step(c, carry):
        m, l, acc = carry
        k0 = pl.multiple_of(c * tk, tk)
        kc = k_ref[pl.ds(k0, tk), :]
        vc = v_ref[pl.ds(k0, tk), :]
        s = lax.dot_general(qq, kc, _NT, preferred_element_type=F32)
        expand = jnp.where(blk_off == c * (tk // SLC_LEN), 1.0, 0.0).astype(BF16)
        picked = jnp.dot(sel, expand, preferred_element_type=F32)
        valid = (picked > 0.5) & (key_off <= q0 - k0)
        s3 = jnp.where(valid[None], s.reshape(NSA_G, tq, tk), NEG_INF)
        s = s3.reshape(rows, tk)
        m_new = jnp.maximum(m, jnp.max(s, axis=-1, keepdims=True))
        m_safe = jnp.where(m_new == NEG_INF, 0.0, m_new)
        alpha = jnp.exp2((m - m_safe) * exp2_scale)
        p = jnp.exp2((s - m_safe) * exp2_scale)
        l = alpha * l + jnp.sum(p, axis=-1, keepdims=True)
        acc = alpha * acc + jnp.dot(p.astype(BF16), vc, preferred_element_type=F32)
        return m_new, l, acc

    init = (jnp.full((rows, 1), NEG_INF, F32), jnp.zeros((rows, 1), F32), jnp.zeros((rows, LANES), F32))
    nchunk = (q0 + tq + tk - 1) // tk
    m, l, acc = lax.fori_loop(0, nchunk, step, init)
    o = acc / jnp.maximum(l, 1e-30)
    for g in range(NSA_G):
        o_ref[:, g * LANES:(g + 1) * LANES] = o[g * tq:(g + 1) * tq]


def nsa_selected(q, k, v, sel, B, S, *, k_col0, v_col0, tq=128, tk=1024):
    T = B * S
    tk = min(tk, S)
    assert S % tk == 0 and tk % SLC_LEN == 0
    nq = S // tq
    ns = S // SLC_LEN
    gw = NSA_G * NSA_DIM
    kb, vb = k_col0 // LANES, v_col0 // LANES
    return pl.pallas_call(
        functools.partial(_nsa_sel_kernel, tq=tq, tk=tk, scale=NSA_DIM ** -0.5),
        grid=(B, NSA_KV, nq),
        in_specs=[pl.BlockSpec((tq, gw), lambda b, h, i: (b * nq + i, h)),
                  pl.BlockSpec((S, LANES), lambda b, h, i: (b, kb + h)),
                  pl.BlockSpec((S, LANES), lambda b, h, i: (b, vb + h)),
                  pl.BlockSpec((1, 1, tq, ns), lambda b, h, i: (b, h, i, 0))],
        out_specs=pl.BlockSpec((tq, gw), lambda b, h, i: (b * nq + i, h)),
        out_shape=jax.ShapeDtypeStruct((T, NSA_HEADS * NSA_DIM), F32),
        compiler_params=_cparams(("arbitrary", "arbitrary", "arbitrary")),
        name="nsa_selected",
    )(q, k, v, sel)


def _nsa_win_kernel(q_ref, k_ref, v_ref, o_ref, *, tq, window, scale):
    i = pl.program_id(2)
    q0 = i * tq
    rows = NSA_G * tq
    span = tq + window
    ks = pl.multiple_of(jnp.maximum(q0 - window, 0), tq)
    qq = _stack_heads(q_ref, tq)
    kc = k_ref[pl.ds(ks, span), :]
    vc = v_ref[pl.ds(ks, span), :]
    s = lax.dot_general(qq, kc, _NT, preferred_element_type=F32)
    r = lax.broadcasted_iota(jnp.int32, (rows, span), 0) % tq
    cc = lax.broadcasted_iota(jnp.int32, (rows, span), 1)
    rel = (q0 - ks) + r - cc
    s = jnp.where((rel >= 0) & (rel < window), s, NEG_INF)
    m = jnp.max(s, axis=-1, keepdims=True)
    p = jnp.exp2((s - m) * (scale * math.log2(math.e)))
    o = jnp.dot(p.astype(BF16), vc, preferred_element_type=F32) / jnp.sum(p, axis=-1, keepdims=True)
    for g in range(NSA_G):
        o_ref[:, g * LANES:(g + 1) * LANES] = o[g * tq:(g + 1) * tq]


def nsa_window(q, k, v, B, S, *, k_col0, v_col0, tq=256):
    T = B * S
    nq = S // tq
    gw = NSA_G * NSA_DIM
    kb, vb = k_col0 // LANES, v_col0 // LANES
    assert S >= tq + NSA_WINDOW
    return pl.pallas_call(
        functools.partial(_nsa_win_kernel, tq=tq, window=NSA_WINDOW, scale=NSA_DIM ** -0.5),
        grid=(B, NSA_KV, nq),
        in_specs=[pl.BlockSpec((tq, gw), lambda b, h, i: (b * nq + i, h)),
                  pl.BlockSpec((S, LANES), lambda b, h, i: (b, kb + h)),
                  pl.BlockSpec((S, LANES), lambda b, h, i: (b, vb + h))],
        out_specs=pl.BlockSpec((tq, gw), lambda b, h, i: (b * nq + i, h)),
        out_shape=jax.ShapeDtypeStruct((T, NSA_HEADS * NSA_DIM), F32),
        compiler_params=_cparams(("arbitrary", "arbitrary", "arbitrary")),
        name="nsa_window",
    )(q, k, v)


def _nsa_mix_kernel(g_ref, oc_ref, os_ref, ow_ref, o_ref):
    sg = jax.nn.sigmoid(g_ref[...])
    for hq in range(NSA_HEADS):
        sl = slice(hq * LANES, (hq + 1) * LANES)
        acc = sg[:, hq:hq + 1] * oc_ref[:, sl]
        acc = acc + sg[:, NSA_HEADS + hq:NSA_HEADS + hq + 1] * os_ref[:, sl]
        acc = acc + sg[:, 2 * NSA_HEADS + hq:2 * NSA_HEADS + hq + 1] * ow_ref[:, sl]
        o_ref[:, sl] = acc.astype(o_ref.dtype)


def nsa_mix(gates, oc, os_, ow, *, tm=512):
    T, N = oc.shape
    spec = pl.BlockSpec((tm, N), lambda i: (i, 0))
    return pl.pallas_call(
        _nsa_mix_kernel,
        grid=(T // tm,),
        in_specs=[pl.BlockSpec((tm, LANES), lambda i: (i, 0)), spec, spec, spec],
        out_specs=spec,
        out_shape=jax.ShapeDtypeStruct((T, N), BF16),
        compiler_params=_cparams(("arbitrary",)),
        name="nsa_mix",
    )(gates, oc, os_, ow)


PEER_MIX_TE = 512
NO_RANK = 127.0
GELU_C0 = math.sqrt(2.0 / math.pi)
GELU_C1 = 0.044715 * GELU_C0


def _extract_sorted(s, out_ref, idx_ref, n):
    rowi = lax.broadcasted_iota(jnp.int32, s.shape, 0).astype(F32)

    def body(a, s):
        m = jnp.max(s, axis=0, keepdims=True)
        idx = jnp.min(jnp.where(s == m, rowi, 1e9), axis=0, keepdims=True)
        out_ref[pl.ds(a, 1), :] = m
        idx_ref[pl.ds(a, 1), :] = idx
        return jnp.where(rowi == idx, NEG_INF, s)

    return lax.fori_loop(0, n, body, s)


def _peer_route_kernel(q_ref, sk_ref, n0_ref, e0_ref, r1_ref, e1_ref, sv01, ix01, best, ixc):
    K = PEER_TOPK
    q = q_ref[...]
    tt = q.shape[0]
    s0 = lax.dot_general(sk_ref[0, 0], q[:, :PEER_HALF], _NT, preferred_element_type=F32)
    s1 = lax.dot_general(sk_ref[0, 1], q[:, PEER_HALF:], _NT, preferred_element_type=F32)
    _extract_sorted(jnp.concatenate([s0, s1], axis=1), sv01, ix01, K)
    a0 = sv01[:, :tt]
    a1 = sv01[:, tt:]
    cand = jnp.concatenate([a0[0:1] + a1] + [a0[a:a + 1] + a1[0:8] for a in range(1, 8)]
                           + [a0[8:16] + a1[0:1]], axis=0)
    left = _extract_sorted(cand, best, ixc, K)
    taken = jnp.where(left == NEG_INF, 1.0, 0.0)
    bv = best[...]
    z = jnp.sum(jnp.exp(bv - bv[0:1]), axis=0, keepdims=True)
    tail = taken[72:80]
    cnts = ([jnp.sum(taken[0:16], axis=0, keepdims=True)]
            + [jnp.sum(taken[8 + 8 * a:16 + 8 * a], axis=0, keepdims=True) for a in range(1, 8)]
            + [tail[a - 8:a - 7] for a in range(8, 16)])
    rowi = lax.broadcasted_iota(jnp.int32, s0.shape, 0).astype(F32)
    n0 = jnp.zeros_like(s0)
    r1 = jnp.full(s1.shape, NO_RANK, F32)
    for a in range(K):
        n0 = jnp.where(rowi == ix01[a:a + 1, :tt], cnts[a], n0)
        r1 = jnp.where(rowi == ix01[a:a + 1, tt:], float(a), r1)
    n0_ref[0] = n0
    e0_ref[0] = jnp.exp(s0 - a0[0:1])
    r1_ref[0] = r1
    e1_ref[0] = 0.5 * jnp.exp(s1 - a1[0:1]) / z


def peer_route(q, subkeys, *, tt=512):
    T = q.shape[0]
    H = PEER_HEADS
    f32 = jax.ShapeDtypeStruct((H, PEER_NKEYS, T), F32)
    bspec = pl.BlockSpec((1, PEER_NKEYS, tt), lambda i, h: (h, 0, i))
    return pl.pallas_call(
        _peer_route_kernel,
        grid=(T // tt, H),
        in_specs=[pl.BlockSpec((tt, 2 * PEER_HALF), lambda i, h: (i, h)),
                  pl.BlockSpec((1, 2, PEER_NKEYS, PEER_HALF), lambda i, h: (h, 0, 0, 0))],
        out_specs=[bspec, bspec, bspec, bspec],
        out_shape=[f32, f32, f32, f32],
        scratch_shapes=[pltpu.VMEM((PEER_TOPK, 2 * tt), F32), pltpu.VMEM((PEER_TOPK, 2 * tt), F32),
                        pltpu.VMEM((PEER_TOPK, tt), F32), pltpu.VMEM((PEER_TOPK, tt), F32)],
        compiler_params=_cparams(("arbitrary", "arbitrary")),
        name="peer_route",
    )(q, subkeys.astype(BF16))


def _peer_mix_kernel(ht_ref, u_ref, vt_ref, n0_ref, e0_ref, r1_ref, e1_ref, x_ref, gate_ref, o_ref,
                     acc_ref, at_ref, cf_ref, *, te, n_expert_steps, out_cols):
    g = pl.program_id(1)
    tt = ht_ref.shape[1]
    nslab = te // PEER_NKEYS
    tile_rows = 64

    def build(row0):
        for ii in range(nslab):
            n0rows = [n0_ref[hh, pl.ds(row0 + ii, 1), :] for hh in range(PEER_HEADS)]
            e0rows = [e0_ref[hh, pl.ds(row0 + ii, 1), :] for hh in range(PEER_HEADS)]
            for ts in range(tt // LANES):
                lanes = slice(ts * LANES, (ts + 1) * LANES)
                for part in range(PEER_NKEYS // tile_rows):
                    jr = slice(part * tile_rows, (part + 1) * tile_rows)
                    er = slice(ii * PEER_NKEYS + part * tile_rows, ii * PEER_NKEYS + (part + 1) * tile_rows)
                    w = None
                    for hh in range(PEER_HEADS):
                        term = jnp.where(r1_ref[hh, jr, lanes] < n0rows[hh][:, lanes],
                                         e0rows[hh][:, lanes] * e1_ref[hh, jr, lanes], 0.0)
                        w = term if w is None else w + term
                    a = at_ref[er, lanes]
                    act = a + a * jnp.tanh(a * (GELU_C0 + GELU_C1 * (a * a)))
                    cf_ref[er, lanes] = (w * act).astype(BF16)

    @pl.when(g == 0)
    def _():
        acc_ref[...] = jnp.zeros_like(acc_ref)

    @pl.when(g < n_expert_steps)
    def _():
        at_ref[...] = jnp.dot(u_ref[...], ht_ref[...], preferred_element_type=F32)
        build(g * nslab)
        acc_ref[...] += jnp.dot(vt_ref[0], cf_ref[...], preferred_element_type=F32)

    @pl.when(g >= n_expert_steps)
    def _():
        c0 = pl.multiple_of((g - n_expert_steps) * out_cols, out_cols)
        o_ref[...] = x_ref[...] + gate_ref[0] * acc_ref[pl.ds(c0, out_cols), :].T


def peer_mix(ht, u, vt, route, x2, gate, S, *, tt=1024, out_cols=512):
    D, T = ht.shape
    E = u.shape[0]
    te = vt.shape[2]
    assert vt.shape == (E // te, D, te)
    H = PEER_HEADS
    tt = min(tt, S)
    per = S // tt
    B = T // S
    ne = E // te
    nfin = D // out_cols
    n0, e0, r1, e1 = route
    once = pl.Buffered(1)
    bspec = pl.BlockSpec((H, PEER_NKEYS, tt), lambda i, g: (0, 0, i), pipeline_mode=once)
    fin = lambda g: jnp.maximum(g - ne, 0)
    return pl.pallas_call(
        functools.partial(_peer_mix_kernel, te=te, n_expert_steps=ne, out_cols=out_cols),
        grid=(T // tt, ne + nfin),
        in_specs=[pl.BlockSpec((D, tt), lambda i, g: (0, i), pipeline_mode=once),
                  pl.BlockSpec((te, D), lambda i, g: (jnp.minimum(g, ne - 1), 0)),
                  pl.BlockSpec((1, D, te), lambda i, g: (jnp.minimum(g, ne - 1), 0, 0)),
                  bspec, bspec, bspec, bspec,
                  pl.BlockSpec((tt, out_cols), lambda i, g: (i, fin(g))),
                  pl.BlockSpec((1, 1, out_cols), lambda i, g: (i // per, 0, fin(g)))],
        out_specs=pl.BlockSpec((tt, out_cols), lambda i, g: (i, fin(g))),
        out_shape=jax.ShapeDtypeStruct((T, D), F32),
        scratch_shapes=[pltpu.VMEM((D, tt), F32), pltpu.VMEM((te, tt), F32), pltpu.VMEM((te, tt), BF16)],
        compiler_params=_cparams(("arbitrary", "arbitrary"), vmem_mb=56),
        name="peer_mix",
    )(ht, u, vt, n0, e0, r1, e1, x2, gate.reshape(B, 1, D))


def _rmsnorm_kernel(x_ref, g_ref, o_ref):
    x = x_ref[...]
    ms = jnp.mean(x * x, axis=-1, keepdims=True)
    o_ref[...] = x * lax.rsqrt(ms + NORM_EPS) * g_ref[...]


def rmsnorm_rows(x2, g, *, tm=1024):
    T, D = x2.shape
    return pl.pallas_call(
        _rmsnorm_kernel,
        grid=(T // tm,),
        in_specs=[pl.BlockSpec((tm, D), lambda i: (i, 0)), pl.BlockSpec((1, D), lambda i: (0, 0))],
        out_specs=pl.BlockSpec((tm, D), lambda i: (i, 0)),
        out_shape=jax.ShapeDtypeStruct((T, D), F32),
        compiler_params=_cparams(("arbitrary",)),
        name="final_rmsnorm",
    )(x2, g.reshape(1, D))


def _even_w_in_layout(w):
    D = w.shape[0]
    da = DA_HEADS * DA_DIM
    qa = w[:, 0:2 * da].reshape(D, 2, DA_HEADS, DA_DIM)
    ka = w[:, 2 * da:4 * da].reshape(D, 2, DA_HEADS, DA_DIM)
    o = 4 * da
    va = w[:, o:o + DA_HEADS * DA_VDIM]
    o += DA_HEADS * DA_VDIM
    qs = w[:, o:o + SW_HEADS * SW_DIM]
    o += SW_HEADS * SW_DIM
    ks = w[:, o:o + SW_KV * SW_DIM].reshape(D, SW_KV, 1, SW_DIM)
    o += SW_KV * SW_DIM
    vs = w[:, o:o + SW_KV * SW_DIM].reshape(D, SW_KV, 1, SW_DIM)
    scale = DA_DIM ** -0.5
    assert scale == 0.125 and SW_DIM == DA_DIM
    qa2 = (qa * scale).transpose(0, 2, 1, 3).reshape(D, 2 * da)
    ka2 = ka.transpose(0, 2, 1, 3).reshape(D, 2 * da)
    ks2 = jnp.broadcast_to(ks, (D, SW_KV, 2, SW_DIM)).reshape(D, 2 * SW_KV * SW_DIM)
    vs2 = jnp.broadcast_to(vs, (D, SW_KV, 2, SW_DIM)).reshape(D, 2 * SW_KV * SW_DIM)
    return jnp.concatenate([qa2, ka2, qs * scale, ks2, va, vs2], axis=1).astype(BF16)


def _odd_w_in_layout(w):
    D = w.shape[0]
    kvw = NSA_KV * NSA_DIM
    o = NSA_HEADS * NSA_DIM
    q = w[:, :o]
    kc, vc, ksl, vsl, kw, vw = [w[:, o + n * kvw:o + (n + 1) * kvw] for n in range(6)]
    gates = w[:, o + 6 * kvw:]
    pad = jnp.zeros((D, 3840 - (o + 6 * kvw + gates.shape[1])), w.dtype)
    return jnp.concatenate([q, ksl, kw, vsl, vw, kc, vc, gates, pad], axis=1).astype(BF16)


def _overlap_t(S):
    ncp = S // CMP_STRIDE
    ns = S // SLC_LEN
    cst = np.arange(ncp)[None, :] * CMP_STRIDE
    sst = np.arange(ns)[:, None] * SLC_LEN
    ov = np.clip(np.minimum(cst + CMP_LEN, sst + SLC_LEN) - np.maximum(cst, sst), 0, None)
    return jnp.asarray(ov.astype(np.float32) / CMP_LEN, dtype=BF16)


def _peer_layer(x2, mod, g, wq, subkeys, u_tab, v_tab, B, S):
    D = x2.shape[1]
    shift, scale, gate = mod[:, :D], mod[:, D:2 * D], mod[:, 2 * D:]
    q, h = norm_mod_matmul(x2, g, scale, shift, wq.astype(BF16), S, emit_h=True, out_dtype=BF16)
    route = peer_route(q, subkeys)
    E = v_tab.shape[0]
    vt = v_tab.astype(BF16).reshape(E // PEER_MIX_TE, PEER_MIX_TE, D).transpose(0, 2, 1)
    return peer_mix(h.T, u_tab.astype(BF16), vt, route, x2, gate, S)


def _even_layer(x2, mod, g, w_in, w_out, lam_vecs, subln_g, sinks, lam_init, cos64, sin64, B, S):
    D = x2.shape[1]
    shift, scale, gate = mod[:, :D], mod[:, D:2 * D], mod[:, 2 * D:]
    y = norm_mod_matmul(x2, g, scale, shift, _even_w_in_layout(w_in), S)
    qk = rope_cols(y, 0, 3584, cos64, sin64, 64, S, cw=1792)
    v = cast_cols(y, 3584, 1536, BF16, S)
    oa = diff_attention(qk, v, lam_vecs, subln_g, B, S, lam_init, q_col0=0, k_col0=1024, v_col0=0)
    ob = swa_attention(qk, v, sinks, B, S, q_col0=2048, k_col0=3072, v_col0=1024)
    o = jnp.concatenate([oa, ob], axis=1)
    return matmul_residual(o, w_out.astype(BF16), x2, gate, S)


def _odd_layer(x2, mod, g, w_in, w_out, cmp_pos, cmp_w1, cmp_w2, cos128, sin128, cos_c, sin_c, B, S):
    D = x2.shape[1]
    shift, scale, gate = mod[:, :D], mod[:, D:2 * D], mod[:, 2 * D:]
    y = norm_mod_matmul(x2, g, scale, shift, _odd_w_in_layout(w_in), S, tn=1280)
    qk = rope_cols(y, 0, 2560, cos128, sin128, 128, S, cw=1280)
    v = cast_cols(y, 2560, 512, BF16, S)
    xc = cast_cols(y, 3072, 512, F32, S)
    gates = cast_cols(y, 3584, 128, F32, S, cw=128)
    kvc = nsa_compress(xc, cmp_pos, cmp_w1, cmp_w2, cos_c, sin_c, B, S)
    oc, sel = nsa_compressed(qk, kvc, _overlap_t(S), B, S)
    os_ = nsa_selected(qk, qk, v, sel, B, S, k_col0=2048, v_col0=0)
    ow = nsa_window(qk, qk, v, B, S, k_col0=2304, v_col0=256)
    o = nsa_mix(gates, oc, os_, ow)
    return matmul_residual(o, w_out.astype(BF16), x2, gate, S)


def kernel(x, c, ada_w, ada_b, norm_g, even_w_in, even_w_out, da_lambda, da_subln, sw_sinks, odd_w_in, odd_w_out, nsa_cmp_pos, nsa_cmp_w1, nsa_cmp_w2, peer_wq, peer_subkeys, peer_u, peer_v, final_g):
    B, S, D = x.shape
    depth = ada_w.shape[0]
    pos = jnp.arange(S)
    cos64, sin64 = _rope_tables(pos, DA_DIM)
    cos128, sin128 = _rope_tables(pos, NSA_DIM)
    cos_c, sin_c = _rope_tables(jnp.arange(S // CMP_STRIDE) * CMP_STRIDE + CMP_LEN - 1, NSA_DIM)
    mods = ada_mod(c, ada_w, ada_b)
    x2 = x.reshape(B * S, D)
    for layer in range(depth):
        j = layer // 2
        if layer % 2 == 0:
            lam_init = 0.8 - 0.6 * math.exp(-0.3 * layer)
            x2 = _even_layer(x2, mods[2 * layer], norm_g[layer, 0], even_w_in[j], even_w_out[j],
                             da_lambda[j], da_subln[j], sw_sinks[j], lam_init, cos64, sin64, B, S)
        else:
            x2 = _odd_layer(x2, mods[2 * layer], norm_g[layer, 0], odd_w_in[j], odd_w_out[j],
                            nsa_cmp_pos[j], nsa_cmp_w1[j], nsa_cmp_w2[j], cos128, sin128, cos_c, sin_c, B, S)
        x2 = _peer_layer(x2, mods[2 * layer + 1], norm_g[layer, 1], peer_wq[layer], peer_subkeys[layer],
                         peer_u[layer], peer_v[layer], B, S)
    return rmsnorm_rows(x2, final_g).reshape(B, S, D)
```

```python
import functools
import math

import jax
import jax.numpy as jnp
import numpy as np
from jax import lax
from jax.experimental import pallas as pl
from jax.experimental.pallas import tpu as pltpu

F32 = jnp.float32
BF16 = jnp.bfloat16
NEG_INF = float("-inf")

ROPE_THETA = 10000.0
NORM_EPS = 1e-6

DA_HEADS = 8
DA_DIM = 64
DA_VDIM = 128
SW_HEADS = 16
SW_KV = 4
SW_DIM = 64
SW_WINDOW = 128

NSA_HEADS = 16
NSA_KV = 2
NSA_G = NSA_HEADS // NSA_KV
NSA_DIM = 128
CMP_LEN = 32
CMP_STRIDE = 16
CMP_HIDDEN = 256
SLC_LEN = 64
SLC_TOPN = 16
NSA_WINDOW = 512

PEER_HEADS = 8
PEER_NKEYS = 128
PEER_TOPK = 16
PEER_HALF = 128

LANES = 128
V7X_VMEM_BYTES = 64 * 1024 * 1024

_NT = (((1,), (1,)), ((), ()))


def _cparams(sem, vmem_mb=48):
    assert vmem_mb * 1024 * 1024 < V7X_VMEM_BYTES
    return pltpu.CompilerParams(dimension_semantics=sem, vmem_limit_bytes=vmem_mb * 1024 * 1024)


def _ada_kernel(c_ref, w_ref, b_ref, o_ref):
    c = c_ref[...]
    sc = c * jax.nn.sigmoid(c)
    o_ref[0] = jnp.dot(sc, w_ref[0], preferred_element_type=F32,
                       precision=lax.Precision.HIGHEST) + b_ref[0]


def ada_mod(c, ada_w, ada_b):
    B, D = c.shape
    n = ada_w.shape[0] * ada_w.shape[1]
    w = ada_w.reshape(n, D, 3 * D)
    b = ada_b.reshape(n, 1, 3 * D)
    rows = 8
    cp = jnp.zeros((rows, D), F32).at[:B].set(c)
    tn = 512
    out = pl.pallas_call(
        _ada_kernel,
        grid=(n, 3 * D // tn),
        in_specs=[pl.BlockSpec((rows, D), lambda l, j: (0, 0)),
                  pl.BlockSpec((1, D, tn), lambda l, j: (l, 0, j)),
                  pl.BlockSpec((1, 1, tn), lambda l, j: (l, 0, j))],
        out_specs=pl.BlockSpec((1, rows, tn), lambda l, j: (l, 0, j)),
        out_shape=jax.ShapeDtypeStruct((n, rows, 3 * D), F32),
        compiler_params=_cparams(("arbitrary", "arbitrary")),
        name="ada_mod",
    )(cp, w, b)
    return out[:, :B]


def _nmm_kernel(x_ref, g_ref, sc_ref, sh_ref, w_ref, *rest, emit_h):
    if emit_h:
        y_ref, h_out_ref, hs_ref = rest
    else:
        y_ref, hs_ref = rest

    @pl.when(pl.program_id(1) == 0)
    def _():
        x = x_ref[...]
        ms = jnp.mean(x * x, axis=-1, keepdims=True)
        y = x * lax.rsqrt(ms + NORM_EPS) * g_ref[...]
        h = (y * (1.0 + sc_ref[0]) + sh_ref[0]).astype(BF16)
        hs_ref[...] = h
        if emit_h:
            h_out_ref[...] = h

    y_ref[...] = jnp.dot(hs_ref[...], w_ref[...], preferred_element_type=F32).astype(y_ref.dtype)


def norm_mod_matmul(x2, g, scale, shift, w, S, *, emit_h=False, out_dtype=F32, tm=1024, tn=1024):
    T, D = x2.shape
    N = w.shape[1]
    tm = min(tm, S)
    assert S % tm == 0 and N % tn == 0
    per = S // tm
    B = T // S
    out_shape = [jax.ShapeDtypeStruct((T, N), out_dtype)]
    out_specs = [pl.BlockSpec((tm, tn), lambda i, j: (i, j))]
    if emit_h:
        out_shape.append(jax.ShapeDtypeStruct((T, D), BF16))
        out_specs.append(pl.BlockSpec((tm, D), lambda i, j: (i, 0)))
    res = pl.pallas_call(
        functools.partial(_nmm_kernel, emit_h=emit_h),
        grid=(T // tm, N // tn),
        in_specs=[pl.BlockSpec((tm, D), lambda i, j: (i, 0)),
                  pl.BlockSpec((1, D), lambda i, j: (0, 0)),
                  pl.BlockSpec((1, 1, D), lambda i, j: (i // per, 0, 0)),
                  pl.BlockSpec((1, 1, D), lambda i, j: (i // per, 0, 0)),
                  pl.BlockSpec((D, tn), lambda i, j: (0, j))],
        out_specs=out_specs,
        out_shape=out_shape,
        scratch_shapes=[pltpu.VMEM((tm, D), BF16)],
        compiler_params=_cparams(("arbitrary", "arbitrary"), vmem_mb=56),
        name="norm_mod_matmul",
    )(x2, g.reshape(1, D), scale.reshape(B, 1, D), shift.reshape(B, 1, D), w)
    return res if emit_h else res[0]


def _mmres_kernel(a_ref, w_ref, x_ref, gate_ref, o_ref):
    y = jnp.dot(a_ref[...], w_ref[...], preferred_element_type=F32)
    o_ref[...] = x_ref[...] + gate_ref[0] * y


def matmul_residual(a, w, x2, gate, S, *, tm=1024, tn=1024):
    T, K = a.shape
    N = w.shape[1]
    tm = min(tm, S)
    per = S // tm
    B = T // S
    return pl.pallas_call(
        _mmres_kernel,
        grid=(T // tm, N // tn),
        in_specs=[pl.BlockSpec((tm, K), lambda i, j: (i, 0)),
                  pl.BlockSpec((K, tn), lambda i, j: (0, j)),
                  pl.BlockSpec((tm, tn), lambda i, j: (i, j)),
                  pl.BlockSpec((1, 1, tn), lambda i, j: (i // per, 0, j))],
        out_specs=pl.BlockSpec((tm, tn), lambda i, j: (i, j)),
        out_shape=jax.ShapeDtypeStruct((T, N), F32),
        compiler_params=_cparams(("arbitrary", "arbitrary")),
        name="matmul_residual",
    )(a, w, x2, gate.reshape(B, 1, N))


def _rope_block(v, cos, sin, hd):
    if hd == 64:
        lane = lax.broadcasted_iota(jnp.int32, v.shape, 1)
        lo = (lane % 64) < 32
        partner = jnp.where(lo, pltpu.roll(v, 96, 1), pltpu.roll(v, 32, 1))
    else:
        partner = pltpu.roll(v, 64, 1)
    return v * cos + partner * sin


def _rope_kernel(y_ref, cos_ref, sin_ref, o_ref, *, hd):
    cos = cos_ref[...]
    sin = sin_ref[...]
    for k in range(o_ref.shape[1] // LANES):
        sl = slice(k * LANES, (k + 1) * LANES)
        o_ref[:, sl] = _rope_block(y_ref[:, sl], cos, sin, hd).astype(o_ref.dtype)


def _cast_kernel(y_ref, o_ref):
    o_ref[...] = y_ref[...].astype(o_ref.dtype)


def rope_cols(y, col0, ncols, cos, sin, hd, S, *, tm=1024, cw=512):
    T = y.shape[0]
    tm = min(tm, S)
    per = S // tm
    assert col0 % cw == 0 and ncols % cw == 0
    c0 = col0 // cw
    return pl.pallas_call(
        functools.partial(_rope_kernel, hd=hd),
        grid=(T // tm, ncols // cw),
        in_specs=[pl.BlockSpec((tm, cw), lambda i, j: (i, c0 + j)),
                  pl.BlockSpec((tm, LANES), lambda i, j: (i % per, 0)),
                  pl.BlockSpec((tm, LANES), lambda i, j: (i % per, 0))],
        out_specs=pl.BlockSpec((tm, cw), lambda i, j: (i, j)),
        out_shape=jax.ShapeDtypeStruct((T, ncols), BF16),
        compiler_params=_cparams(("arbitrary", "arbitrary")),
        name="rope_cols",
    )(y, cos, sin)


def cast_cols(y, col0, ncols, dtype, S, *, tm=1024, cw=512):
    T = y.shape[0]
    tm = min(tm, S)
    cw = min(cw, ncols)
    assert col0 % cw == 0 and ncols % cw == 0
    c0 = col0 // cw
    return pl.pallas_call(
        _cast_kernel,
        grid=(T // tm, ncols // cw),
        in_specs=[pl.BlockSpec((tm, cw), lambda i, j: (i, c0 + j))],
        out_specs=pl.BlockSpec((tm, cw), lambda i, j: (i, j)),
        out_shape=jax.ShapeDtypeStruct((T, ncols), dtype),
        compiler_params=_cparams(("arbitrary", "arbitrary")),
        name="cast_cols",
    )(y)


def _rope_tables(pos, hd):
    inv = jnp.power(ROPE_THETA, -jnp.arange(0, hd, 2, dtype=F32) / hd)
    ang = pos.astype(F32)[:, None] * inv[None, :]
    cos, sin = jnp.cos(ang), jnp.sin(ang)
    reps = LANES // hd
    cos_l = jnp.tile(jnp.concatenate([cos, cos], axis=1), (1, reps))
    sin_l = jnp.tile(jnp.concatenate([-sin, sin], axis=1), (1, reps))
    return cos_l, sin_l


def _split_halves(q):
    lane = lax.broadcasted_iota(jnp.int32, q.shape, 1)
    zero = jnp.zeros_like(q)
    return jnp.concatenate([jnp.where(lane < 64, q, zero), jnp.where(lane >= 64, q, zero)], axis=0)


def _diff_kernel(q_ref, k_ref, v_ref, lv_ref, sg_ref, o_ref, *, tq, tk, lam_init):
    i = pl.program_id(2)
    q0 = i * tq
    qq = _split_halves(q_ref[...])
    rows = 2 * tq

    def step(c, carry, masked):
        m, l, acc = carry
        k0 = pl.multiple_of(c * tk, tk)
        kc = k_ref[pl.ds(k0, tk), :]
        vc = v_ref[pl.ds(k0, tk), :]
        s = lax.dot_general(qq, kc, _NT, preferred_element_type=F32)
        if masked:
            r = lax.broadcasted_iota(jnp.int32, (rows, tk), 0) % tq
            cc = lax.broadcasted_iota(jnp.int32, (rows, tk), 1)
            s = jnp.where(cc - r <= q0 - k0, s, NEG_INF)
        m_new = jnp.maximum(m, jnp.max(s, axis=-1, keepdims=True))
        alpha = jnp.exp(m - m_new)
        p = jnp.exp(s - m_new)
        l = alpha * l + jnp.sum(p, axis=-1, keepdims=True)
        acc = alpha * acc + jnp.dot(p.astype(BF16), vc, preferred_element_type=F32)
        return m_new, l, acc

    init = (jnp.full((rows, 1), NEG_INF, F32), jnp.zeros((rows, 1), F32), jnp.zeros((rows, LANES), F32))
    nfull = q0 // tk
    carry = lax.fori_loop(0, nfull, functools.partial(step, masked=False), init)
    m, l, acc = step(nfull, carry, True)
    o = acc / l
    lv = lv_ref[...]
    lam = (jnp.exp(jnp.sum(lv[0:1] * lv[1:2], axis=-1, keepdims=True))
           - jnp.exp(jnp.sum(lv[2:3] * lv[3:4], axis=-1, keepdims=True)) + lam_init)
    o = o[:tq] - lam * o[tq:]
    o = o * lax.rsqrt(jnp.mean(o * o, axis=-1, keepdims=True) + NORM_EPS) * sg_ref[...]
    o_ref[...] = (o * (1.0 - lam_init)).astype(o_ref.dtype)


def diff_attention(qk, v, lam_vecs, subln_g, B, S, lam_init, *, q_col0, k_col0, v_col0, tq=512, tk=1024):
    T = B * S
    tq = min(tq, S)
    tk = min(tk, S)
    assert tk % tq == 0 and S % tk == 0
    nq = S // tq
    H = DA_HEADS
    qb, kb, vb = q_col0 // LANES, k_col0 // LANES, v_col0 // LANES
    return pl.pallas_call(
        functools.partial(_diff_kernel, tq=tq, tk=tk, lam_init=lam_init),
        grid=(B, H, nq),
        in_specs=[pl.BlockSpec((tq, LANES), lambda b, h, i: (b * nq + i, qb + h)),
                  pl.BlockSpec((S, LANES), lambda b, h, i: (b, kb + h)),
                  pl.BlockSpec((S, LANES), lambda b, h, i: (b, vb + h)),
                  pl.BlockSpec((4, DA_DIM), lambda b, h, i: (0, 0)),
                  pl.BlockSpec((1, DA_VDIM), lambda b, h, i: (0, 0))],
        out_specs=pl.BlockSpec((tq, LANES), lambda b, h, i: (b * nq + i, h)),
        out_shape=jax.ShapeDtypeStruct((T, H * DA_VDIM), BF16),
        compiler_params=_cparams(("arbitrary", "arbitrary", "arbitrary")),
        name="diff_attention",
    )(qk, qk, v, lam_vecs.astype(F32), subln_g.reshape(1, DA_VDIM).astype(F32))


def _swa_kernel(sink_ref, q_ref, k_ref, v_ref, o_ref, *, tq, window):
    pr = pl.program_id(1)
    i = pl.program_id(2)
    q0 = i * tq
    span = tq + window
    ks = pl.multiple_of(jnp.maximum(q0 - window, 0), window)
    qq = _split_halves(q_ref[...])
    rows = 2 * tq
    kc = k_ref[pl.ds(ks, span), :]
    vc = v_ref[pl.ds(ks, span), :]
    s = lax.dot_general(qq, kc, _NT, preferred_element_type=F32)
    r = lax.broadcasted_iota(jnp.int32, (rows, span), 0) % tq
    cc = lax.broadcasted_iota(jnp.int32, (rows, span), 1)
    rel = (q0 - ks) + r - cc
    s = jnp.where((rel >= 0) & (rel < window), s, NEG_INF)
    rr = lax.broadcasted_iota(jnp.int32, (rows, 1), 0)
    sink = jnp.where(rr < tq, sink_ref[2 * pr], sink_ref[2 * pr + 1])
    m = jnp.maximum(jnp.max(s, axis=-1, keepdims=True), sink)
    p = jnp.exp(s - m)
    den = jnp.sum(p, axis=-1, keepdims=True) + jnp.exp(sink - m)
    o2 = jnp.dot(p.astype(BF16), vc, preferred_element_type=F32) / den
    lane = lax.broadcasted_iota(jnp.int32, (tq, LANES), 1)
    o_ref[...] = jnp.where(lane < 64, o2[:tq], o2[tq:]).astype(o_ref.dtype)


def swa_attention(qk, v, sinks, B, S, *, q_col0, k_col0, v_col0, tq=256):
    T = B * S
    tq = min(tq, S)
    nq = S // tq
    npair = SW_HEADS // 2
    ppk = (SW_HEADS // SW_KV) // 2
    qb, kb, vb = q_col0 // LANES, k_col0 // LANES, v_col0 // LANES
    return pl.pallas_call(
        functools.partial(_swa_kernel, tq=tq, window=SW_WINDOW),
        grid=(B, npair, nq),
        in_specs=[pl.BlockSpec(memory_space=pltpu.SMEM),
                  pl.BlockSpec((tq, LANES), lambda b, p, i: (b * nq + i, qb + p)),
                  pl.BlockSpec((S, LANES), lambda b, p, i: (b, kb + p // ppk)),
                  pl.BlockSpec((S, LANES), lambda b, p, i: (b, vb + p // ppk))],
        out_specs=pl.BlockSpec((tq, LANES), lambda b, p, i: (b * nq + i, p)),
        out_shape=jax.ShapeDtypeStruct((T, SW_HEADS * SW_DIM), BF16),
        compiler_params=_cparams(("arbitrary", "arbitrary", "arbitrary")),
        name="swa_attention",
    )(sinks.astype(F32), qk, qk, v)


def _compress_kernel(x_ref, pos_ref, w1a_ref, w1b_ref, w2_ref, cos_ref, sin_ref, o_ref, acc_a, acc_b):
    tok = pl.program_id(1)
    ntok = pl.num_programs(1)

    @pl.when(tok == 0)
    def _():
        acc_a[...] = jnp.zeros_like(acc_a)
        acc_b[...] = jnp.zeros_like(acc_b)

    for kv in range(2):
        pa = pos_ref[kv, pl.ds(tok, 1), :]
        pb = pos_ref[kv, pl.ds(CMP_STRIDE + tok, 1), :]
        for h in range(NSA_KV):
            idx = kv * NSA_KV + h
            xs = x_ref[0, :, idx * LANES:(idx + 1) * LANES]
            acc_a[idx] += jnp.dot((xs + pa).astype(BF16), w1a_ref[kv], preferred_element_type=F32)
            acc_b[idx] += jnp.dot((xs + pb).astype(BF16), w1b_ref[kv], preferred_element_type=F32)

    @pl.when(tok == ntok - 1)
    def _():
        ncp = acc_a.shape[1]
        for kv in range(2):
            for h in range(NSA_KV):
                idx = kv * NSA_KV + h
                hid = acc_a[idx] + pltpu.roll(acc_b[idx], ncp - 1, 0)
                out = jnp.dot(jax.nn.gelu(hid).astype(BF16), w2_ref[kv], preferred_element_type=F32)
                if kv == 0:
                    out = _rope_block(out, cos_ref[...], sin_ref[...], NSA_DIM)
                o_ref[0, idx] = out.astype(o_ref.dtype)


def nsa_compress(xc, cmp_pos, cmp_w1, cmp_w2, cos_c, sin_c, B, S):
    ncp = S // CMP_STRIDE
    width = 2 * NSA_KV * NSA_DIM
    x3 = xc.reshape(B, ncp, CMP_STRIDE * width)
    half = CMP_STRIDE * NSA_DIM
    w1 = cmp_w1.astype(BF16)
    return pl.pallas_call(
        _compress_kernel,
        grid=(B, CMP_STRIDE),
        in_specs=[pl.BlockSpec((1, ncp, width), lambda b, t: (b, 0, t)),
                  pl.BlockSpec((2, CMP_LEN, NSA_DIM), lambda b, t: (0, 0, 0)),
                  pl.BlockSpec((2, NSA_DIM, CMP_HIDDEN), lambda b, t: (0, t, 0)),
                  pl.BlockSpec((2, NSA_DIM, CMP_HIDDEN), lambda b, t: (0, CMP_STRIDE + t, 0)),
                  pl.BlockSpec((2, CMP_HIDDEN, NSA_DIM), lambda b, t: (0, 0, 0)),
                  pl.BlockSpec((ncp, LANES), lambda b, t: (0, 0)),
                  pl.BlockSpec((ncp, LANES), lambda b, t: (0, 0))],
        out_specs=pl.BlockSpec((1, 2 * NSA_KV, ncp, NSA_DIM), lambda b, t: (b, 0, 0, 0)),
        out_shape=jax.ShapeDtypeStruct((B, 2 * NSA_KV, ncp, NSA_DIM), BF16),
        scratch_shapes=[pltpu.VMEM((2 * NSA_KV, ncp, CMP_HIDDEN), F32),
                        pltpu.VMEM((2 * NSA_KV, ncp, CMP_HIDDEN), F32)],
        compiler_params=_cparams(("arbitrary", "arbitrary")),
        name="nsa_compress",
    )(x3, cmp_pos.astype(F32), w1, w1, cmp_w2.astype(BF16), cos_c, sin_c)


def _stack_heads(q_ref, tq):
    return jnp.concatenate([q_ref[:, g * LANES:(g + 1) * LANES] for g in range(NSA_G)], axis=0)


def _nsa_cmp_kernel(q_ref, kc_ref, vc_ref, ovt_ref, o_ref, sel_ref, *, tq, scale):
    i = pl.program_id(2)
    q0 = i * tq
    rows = NSA_G * tq
    ncp = kc_ref.shape[2]
    ns = ovt_ref.shape[0]
    qq = _stack_heads(q_ref, tq)
    s = lax.dot_general(qq, kc_ref[0, 0], _NT, preferred_element_type=F32) * scale
    tpos = q0 + lax.broadcasted_iota(jnp.int32, (rows, ncp), 0) % tq
    cend = lax.broadcasted_iota(jnp.int32, (rows, ncp), 1) * CMP_STRIDE + (CMP_LEN - 1)
    s = jnp.where(cend <= tpos, s, NEG_INF)
    m = jnp.max(s, axis=-1, keepdims=True)
    m = jnp.where(m == NEG_INF, 0.0, m)
    p = jnp.exp(s - m)
    p = p / jnp.maximum(jnp.sum(p, axis=-1, keepdims=True), 1e-30)
    o = jnp.dot(p.astype(BF16), vc_ref[0, 0], preferred_element_type=F32)
    for g in range(NSA_G):
        o_ref[:, g * LANES:(g + 1) * LANES] = o[g * tq:(g + 1) * tq]
    psum = p[0:tq]
    for g in range(1, NSA_G):
        psum = psum + p[g * tq:(g + 1) * tq]
    p_hi = psum.astype(BF16)
    p_lo = (psum - p_hi.astype(F32)).astype(BF16)
    ovt = ovt_ref[...]
    imp = (lax.dot_general(ovt, p_hi, _NT, preferred_element_type=F32)
           + lax.dot_general(ovt, p_lo, _NT, preferred_element_type=F32))
    blk = lax.broadcasted_iota(jnp.int32, (ns, tq), 0)
    cur = (q0 + lax.broadcasted_iota(jnp.int32, (ns, tq), 1)) // SLC_LEN
    future = blk > cur
    forced = (blk == 0) | (blk == cur) | (blk == cur - 1)
    imp = jnp.where(forced, jnp.inf, imp)
    imp = jnp.where(future, NEG_INF, imp)
    sub = 8
    groups = [imp[g * sub:(g + 1) * sub] for g in range(ns // sub)]
    cnts = [jnp.zeros((sub, tq), F32) for _ in groups]
    below = lax.broadcasted_iota(jnp.int32, (sub, tq), 0)
    for sp in range(ns):
        row = imp[sp:sp + 1]
        for g, vals in enumerate(groups):
            if (g + 1) * sub - 1 < sp:
                beats = row > vals
            elif g * sub > sp:
                beats = row >= vals
            else:
                beats = (row > vals) | ((row == vals) & (below > sp - g * sub))
            cnts[g] = cnts[g] + jnp.where(beats, 1.0, 0.0)
    cnt = jnp.concatenate(cnts, axis=0)
    sel_t = jnp.where((cnt < float(SLC_TOPN)) & jnp.logical_not(future), 1.0, 0.0)
    sel_ref[0, 0] = sel_t.T.astype(sel_ref.dtype)


def nsa_compressed(q, kvc, ovt, B, S, *, tq=128):
    T = B * S
    nq = S // tq
    ncp = S // CMP_STRIDE
    ns = S // SLC_LEN
    gw = NSA_G * NSA_DIM
    return pl.pallas_call(
        functools.partial(_nsa_cmp_kernel, tq=tq, scale=NSA_DIM ** -0.5),
        grid=(B, NSA_KV, nq),
        in_specs=[pl.BlockSpec((tq, gw), lambda b, h, i: (b * nq + i, h)),
                  pl.BlockSpec((1, 1, ncp, NSA_DIM), lambda b, h, i: (b, h, 0, 0)),
                  pl.BlockSpec((1, 1, ncp, NSA_DIM), lambda b, h, i: (b, NSA_KV + h, 0, 0)),
                  pl.BlockSpec((ns, ncp), lambda b, h, i: (0, 0))],
        out_specs=[pl.BlockSpec((tq, gw), lambda b, h, i: (b * nq + i, h)),
                   pl.BlockSpec((1, 1, tq, ns), lambda b, h, i: (b, h, i, 0))],
        out_shape=[jax.ShapeDtypeStruct((T, NSA_HEADS * NSA_DIM), F32),
                   jax.ShapeDtypeStruct((B, NSA_KV, S, ns), BF16)],
        compiler_params=_cparams(("arbitrary", "arbitrary", "arbitrary")),
        name="nsa_compressed",
    )(q, kvc, kvc, ovt)


def _nsa_sel_kernel(q_ref, k_ref, v_ref, sel_ref, o_ref, *, tq, tk, scale):
    i = pl.program_id(2)
    q0 = i * tq
    rows = NSA_G * tq
    ns = sel_ref.shape[3]
    qq = _stack_heads(q_ref, tq)
    sel = sel_ref[0, 0]

    blk_off = (lax.broadcasted_iota(jnp.int32, (ns, tk), 0)
               - lax.broadcasted_iota(jnp.int32, (ns, tk), 1) // SLC_LEN)
    key_off = lax.broadcasted_iota(jnp.int32, (tq, tk), 1) - lax.broadcasted_iota(jnp.int32, (tq, tk), 0)

    exp2_scale = scale * math.log2(math.e)

    def step(c, carry):
        m, acc = carry
        k0 = pl.multiple_of(c * tk, tk)
        kc = k_ref[pl.ds(k0, tk), :]
        vc = v_ref[pl.ds(k0, tk), :]
        s = lax.dot_general(qq, kc, _NT, preferred_element_type=F32)
        expand = jnp.where(blk_off == c * (tk // SLC_LEN), 1.0, 0.0).astype(BF16)
        picked = jnp.dot(sel, expand, preferred_element_type=F32)
        valid = (picked > 0.5) & (key_off <= q0 - k0)
        s3 = jnp.where(valid[None], s.reshape(NSA_G, tq, tk), NEG_INF)
        s = s3.reshape(rows, tk)
        m_new = jnp.maximum(m, jnp.max(s, axis=-1, keepdims=True))
        m_safe = jnp.where(m_new == NEG_INF, 0.0, m_new)
        alpha = jnp.exp2((m - m_safe) * exp2_scale)
        p = jnp.exp2((s - m_safe) * exp2_scale)
        v_ones = jnp.concatenate([vc, jnp.ones_like(vc)], axis=1)
        acc = alpha * acc + jnp.dot(p.astype(BF16), v_ones, preferred_element_type=F32)
        return m_new, acc

    init = (jnp.full((rows, 1), NEG_INF, F32), jnp.zeros((rows, 2 * LANES), F32))
    nchunk = (q0 + tq + tk - 1) // tk
    m, acc = lax.fori_loop(0, nchunk, step, init)
    o = acc[:, :LANES] / jnp.maximum(acc[:, LANES:LANES + 1], 1e-30)
    for g in range(NSA_G):
        o_ref[:, g * LANES:(g + 1) * LANES] = o[g * tq:(g + 1) * tq]


def nsa_selected(q, k, v, sel, B, S, *, k_col0, v_col0, tq=128, tk=1024):
    T = B * S
    tk = min(tk, S)
    assert S % tk == 0 and tk % SLC_LEN == 0
    nq = S // tq
    ns = S // SLC_LEN
    gw = NSA_G * NSA_DIM
    kb, vb = k_col0 // LANES, v_col0 // LANES
    return pl.pallas_call(
        functools.partial(_nsa_sel_kernel, tq=tq, tk=tk, scale=NSA_DIM ** -0.5),
        grid=(B, NSA_KV, nq),
        in_specs=[pl.BlockSpec((tq, gw), lambda b, h, i: (b * nq + i, h)),
                  pl.BlockSpec((S, LANES), lambda b, h, i: (b, kb + h)),
                  pl.BlockSpec((S, LANES), lambda b, h, i: (b, vb + h)),
                  pl.BlockSpec((1, 1, tq, ns), lambda b, h, i: (b, h, i, 0))],
        out_specs=pl.BlockSpec((tq, gw), lambda b, h, i: (b * nq + i, h)),
        out_shape=jax.ShapeDtypeStruct((T, NSA_HEADS * NSA_DIM), F32),
        compiler_params=_cparams(("arbitrary", "arbitrary", "arbitrary")),
        name="nsa_selected",
    )(q, k, v, sel)


def _nsa_win_kernel(q_ref, k_ref, v_ref, o_ref, *, tq, window, scale):
    i = pl.program_id(2)
    q0 = i * tq
    rows = NSA_G * tq
    span = tq + window
    ks = pl.multiple_of(jnp.maximum(q0 - window, 0), tq)
    qq = _stack_heads(q_ref, tq)
    kc = k_ref[pl.ds(ks, span), :]
    vc = v_ref[pl.ds(ks, span), :]
    s = lax.dot_general(qq, kc, _NT, preferred_element_type=F32)
    r = lax.broadcasted_iota(jnp.int32, (rows, span), 0) % tq
    cc = lax.broadcasted_iota(jnp.int32, (rows, span), 1)
    rel = (q0 - ks) + r - cc
    s = jnp.where((rel >= 0) & (rel < window), s, NEG_INF)
    m = jnp.max(s, axis=-1, keepdims=True)
    p = jnp.exp2((s - m) * (scale * math.log2(math.e)))
    o = jnp.dot(p.astype(BF16), vc, preferred_element_type=F32) / jnp.sum(p, axis=-1, keepdims=True)
    for g in range(NSA_G):
        o_ref[:, g * LANES:(g + 1) * LANES] = o[g * tq:(g + 1) * tq]


def nsa_window(q, k, v, B, S, *, k_col0, v_col0, tq=128):
    T = B * S
    nq = S // tq
    gw = NSA_G * NSA_DIM
    kb, vb = k_col0 // LANES, v_col0 // LANES
    assert S >= tq + NSA_WINDOW
    return pl.pallas_call(
        functools.partial(_nsa_win_kernel, tq=tq, window=NSA_WINDOW, scale=NSA_DIM ** -0.5),
        grid=(B, NSA_KV, nq),
        in_specs=[pl.BlockSpec((tq, gw), lambda b, h, i: (b * nq + i, h)),
                  pl.BlockSpec((S, LANES), lambda b, h, i: (b, kb + h)),
                  pl.BlockSpec((S, LANES), lambda b, h, i: (b, vb + h))],
        out_specs=pl.BlockSpec((tq, gw), lambda b, h, i: (b * nq + i, h)),
        out_shape=jax.ShapeDtypeStruct((T, NSA_HEADS * NSA_DIM), F32),
        compiler_params=_cparams(("arbitrary", "arbitrary", "arbitrary")),
        name="nsa_window",
    )(q, k, v)


def _nsa_mix_kernel(g_ref, oc_ref, os_ref, ow_ref, o_ref):
    sg = jax.nn.sigmoid(g_ref[...])
    for hq in range(NSA_HEADS):
        sl = slice(hq * LANES, (hq + 1) * LANES)
        acc = sg[:, hq:hq + 1] * oc_ref[:, sl]
        acc = acc + sg[:, NSA_HEADS + hq:NSA_HEADS + hq + 1] * os_ref[:, sl]
        acc = acc + sg[:, 2 * NSA_HEADS + hq:2 * NSA_HEADS + hq + 1] * ow_ref[:, sl]
        o_ref[:, sl] = acc.astype(o_ref.dtype)


def nsa_mix(gates, oc, os_, ow, *, tm=512):
    T, N = oc.shape
    spec = pl.BlockSpec((tm, N), lambda i: (i, 0))
    return pl.pallas_call(
        _nsa_mix_kernel,
        grid=(T // tm,),
        in_specs=[pl.BlockSpec((tm, LANES), lambda i: (i, 0)), spec, spec, spec],
        out_specs=spec,
        out_shape=jax.ShapeDtypeStruct((T, N), BF16),
        compiler_params=_cparams(("arbitrary",)),
        name="nsa_mix",
    )(gates, oc, os_, ow)


PEER_MIX_TE = 512
NO_RANK = 127.0
GELU_C0 = math.sqrt(2.0 / math.pi)
GELU_C1 = 0.044715 * GELU_C0


def _extract_sorted(s, out_ref, idx_ref, n):
    rowi = lax.broadcasted_iota(jnp.int32, s.shape, 0).astype(F32)

    def body(a, s):
        m = jnp.max(s, axis=0, keepdims=True)
        idx = jnp.min(jnp.where(s == m, rowi, 1e9), axis=0, keepdims=True)
        out_ref[pl.ds(a, 1), :] = m
        idx_ref[pl.ds(a, 1), :] = idx
        return jnp.where(rowi == idx, NEG_INF, s)

    return lax.fori_loop(0, n, body, s)


def _peer_route_kernel(q_ref, sk_ref, n0_ref, e0_ref, r1_ref, e1_ref, sv01, ix01, best, ixc):
    K = PEER_TOPK
    q = q_ref[...]
    tt = q.shape[0]
    s0 = lax.dot_general(sk_ref[0, 0], q[:, :PEER_HALF], _NT, preferred_element_type=F32)
    s1 = lax.dot_general(sk_ref[0, 1], q[:, PEER_HALF:], _NT, preferred_element_type=F32)
    _extract_sorted(jnp.concatenate([s0, s1], axis=1), sv01, ix01, K)
    a0 = sv01[:, :tt]
    a1 = sv01[:, tt:]
    cand = jnp.concatenate([a0[0:1] + a1] + [a0[a:a + 1] + a1[0:8] for a in range(1, 8)]
                           + [a0[8:16] + a1[0:1]], axis=0)
    left = _extract_sorted(cand, best, ixc, K)
    taken = jnp.where(left == NEG_INF, 1.0, 0.0)
    bv = best[...]
    z = jnp.sum(jnp.exp(bv - bv[0:1]), axis=0, keepdims=True)
    tail = taken[72:80]
    cnts = ([jnp.sum(taken[0:16], axis=0, keepdims=True)]
            + [jnp.sum(taken[8 + 8 * a:16 + 8 * a], axis=0, keepdims=True) for a in range(1, 8)]
            + [tail[a - 8:a - 7] for a in range(8, 16)])
    rowi = lax.broadcasted_iota(jnp.int32, s0.shape, 0).astype(F32)
    n0 = jnp.zeros_like(s0)
    r1 = jnp.full(s1.shape, NO_RANK, F32)
    for a in range(K):
        n0 = jnp.where(rowi == ix01[a:a + 1, :tt], cnts[a], n0)
        r1 = jnp.where(rowi == ix01[a:a + 1, tt:], float(a), r1)
    n0_ref[0] = n0
    e0_ref[0] = jnp.exp(s0 - a0[0:1])
    r1_ref[0] = r1
    e1_ref[0] = 0.5 * jnp.exp(s1 - a1[0:1]) / z


def peer_route(q, subkeys, *, tt=512):
    T = q.shape[0]
    H = PEER_HEADS
    f32 = jax.ShapeDtypeStruct((H, PEER_NKEYS, T), F32)
    bspec = pl.BlockSpec((1, PEER_NKEYS, tt), lambda i, h: (h, 0, i))
    return pl.pallas_call(
        _peer_route_kernel,
        grid=(T // tt, H),
        in_specs=[pl.BlockSpec((tt, 2 * PEER_HALF), lambda i, h: (i, h)),
                  pl.BlockSpec((1, 2, PEER_NKEYS, PEER_HALF), lambda i, h: (h, 0, 0, 0))],
        out_specs=[bspec, bspec, bspec, bspec],
        out_shape=[f32, f32, f32, f32],
        scratch_shapes=[pltpu.VMEM((PEER_TOPK, 2 * tt), F32), pltpu.VMEM((PEER_TOPK, 2 * tt), F32),
                        pltpu.VMEM((PEER_TOPK, tt), F32), pltpu.VMEM((PEER_TOPK, tt), F32)],
        compiler_params=_cparams(("arbitrary", "arbitrary")),
        name="peer_route",
    )(q, subkeys.astype(BF16))


def _peer_mix_kernel(ht_ref, u_ref, vt_ref, n0_ref, e0_ref, r1_ref, e1_ref, x_ref, gate_ref, o_ref,
                     acc_ref, at_ref, cf_ref, *, te, n_expert_steps, out_cols):
    g = pl.program_id(1)
    tt = ht_ref.shape[1]
    nslab = te // PEER_NKEYS
    tile_rows = 64

    def build(row0):
        for ii in range(nslab):
            n0rows = [n0_ref[hh, pl.ds(row0 + ii, 1), :] for hh in range(PEER_HEADS)]
            e0rows = [e0_ref[hh, pl.ds(row0 + ii, 1), :] for hh in range(PEER_HEADS)]
            for ts in range(tt // LANES):
                lanes = slice(ts * LANES, (ts + 1) * LANES)
                for part in range(PEER_NKEYS // tile_rows):
                    jr = slice(part * tile_rows, (part + 1) * tile_rows)
                    er = slice(ii * PEER_NKEYS + part * tile_rows, ii * PEER_NKEYS + (part + 1) * tile_rows)
                    w = None
                    for hh in range(PEER_HEADS):
                        term = jnp.where(r1_ref[hh, jr, lanes] < n0rows[hh][:, lanes],
                                         e0rows[hh][:, lanes] * e1_ref[hh, jr, lanes], 0.0)
                        w = term if w is None else w + term
                    a = at_ref[er, lanes]
                    act = a + a * jnp.tanh(a * (GELU_C0 + GELU_C1 * (a * a)))
                    cf_ref[er, lanes] = (w * act).astype(BF16)

    @pl.when(g == 0)
    def _():
        acc_ref[...] = jnp.zeros_like(acc_ref)

    @pl.when(g < n_expert_steps)
    def _():
        at_ref[...] = jnp.dot(u_ref[...], ht_ref[...], preferred_element_type=F32)
        build(g * nslab)
        acc_ref[...] += jnp.dot(vt_ref[0], cf_ref[...], preferred_element_type=F32)

    @pl.when(g >= n_expert_steps)
    def _():
        c0 = pl.multiple_of((g - n_expert_steps) * out_cols, out_cols)
        o_ref[...] = x_ref[...] + gate_ref[0] * acc_ref[pl.ds(c0, out_cols), :].T


def peer_mix(ht, u, vt, route, x2, gate, S, *, tt=1024, out_cols=512):
    D, T = ht.shape
    E = u.shape[0]
    te = vt.shape[2]
    assert vt.shape == (E // te, D, te)
    H = PEER_HEADS
    tt = min(tt, S)
    per = S // tt
    B = T // S
    ne = E // te
    nfin = D // out_cols
    n0, e0, r1, e1 = route
    once = pl.Buffered(1)
    bspec = pl.BlockSpec((H, PEER_NKEYS, tt), lambda i, g: (0, 0, i), pipeline_mode=once)
    fin = lambda g: jnp.maximum(g - ne, 0)
    return pl.pallas_call(
        functools.partial(_peer_mix_kernel, te=te, n_expert_steps=ne, out_cols=out_cols),
        grid=(T // tt, ne + nfin),
        in_specs=[pl.BlockSpec((D, tt), lambda i, g: (0, i), pipeline_mode=once),
                  pl.BlockSpec((te, D), lambda i, g: (jnp.minimum(g, ne - 1), 0)),
                  pl.BlockSpec((1, D, te), lambda i, g: (jnp.minimum(g, ne - 1), 0, 0)),
                  bspec, bspec, bspec, bspec,
                  pl.BlockSpec((tt, out_cols), lambda i, g: (i, fin(g))),
                  pl.BlockSpec((1, 1, out_cols), lambda i, g: (i // per, 0, fin(g)))],
        out_specs=pl.BlockSpec((tt, out_cols), lambda i, g: (i, fin(g))),
        out_shape=jax.ShapeDtypeStruct((T, D), F32),
        scratch_shapes=[pltpu.VMEM((D, tt), F32), pltpu.VMEM((te, tt), F32), pltpu.VMEM((te, tt), BF16)],
        compiler_params=_cparams(("arbitrary", "arbitrary"), vmem_mb=56),
        name="peer_mix",
    )(ht, u, vt, n0, e0, r1, e1, x2, gate.reshape(B, 1, D))


def _rmsnorm_kernel(x_ref, g_ref, o_ref):
    x = x_ref[...]
    ms = jnp.mean(x * x, axis=-1, keepdims=True)
    o_ref[...] = x * lax.rsqrt(ms + NORM_EPS) * g_ref[...]


def rmsnorm_rows(x2, g, *, tm=1024):
    T, D = x2.shape
    return pl.pallas_call(
        _rmsnorm_kernel,
        grid=(T // tm,),
        in_specs=[pl.BlockSpec((tm, D), lambda i: (i, 0)), pl.BlockSpec((1, D), lambda i: (0, 0))],
        out_specs=pl.BlockSpec((tm, D), lambda i: (i, 0)),
        out_shape=jax.ShapeDtypeStruct((T, D), F32),
        compiler_params=_cparams(("arbitrary",)),
        name="final_rmsnorm",
    )(x2, g.reshape(1, D))


def _even_w_in_layout(w):
    D = w.shape[0]
    da = DA_HEADS * DA_DIM
    qa = w[:, 0:2 * da].reshape(D, 2, DA_HEADS, DA_DIM)
    ka = w[:, 2 * da:4 * da].reshape(D, 2, DA_HEADS, DA_DIM)
    o = 4 * da
    va = w[:, o:o + DA_HEADS * DA_VDIM]
    o += DA_HEADS * DA_VDIM
    qs = w[:, o:o + SW_HEADS * SW_DIM]
    o += SW_HEADS * SW_DIM
    ks = w[:, o:o + SW_KV * SW_DIM].reshape(D, SW_KV, 1, SW_DIM)
    o += SW_KV * SW_DIM
    vs = w[:, o:o + SW_KV * SW_DIM].reshape(D, SW_KV, 1, SW_DIM)
    scale = DA_DIM ** -0.5
    assert scale == 0.125 and SW_DIM == DA_DIM
    qa2 = (qa * scale).transpose(0, 2, 1, 3).reshape(D, 2 * da)
    ka2 = ka.transpose(0, 2, 1, 3).reshape(D, 2 * da)
    ks2 = jnp.broadcast_to(ks, (D, SW_KV, 2, SW_DIM)).reshape(D, 2 * SW_KV * SW_DIM)
    vs2 = jnp.broadcast_to(vs, (D, SW_KV, 2, SW_DIM)).reshape(D, 2 * SW_KV * SW_DIM)
    return jnp.concatenate([qa2, ka2, qs * scale, ks2, va, vs2], axis=1).astype(BF16)


def _odd_w_in_layout(w):
    D = w.shape[0]
    kvw = NSA_KV * NSA_DIM
    o = NSA_HEADS * NSA_DIM
    q = w[:, :o]
    kc, vc, ksl, vsl, kw, vw = [w[:, o + n * kvw:o + (n + 1) * kvw] for n in range(6)]
    gates = w[:, o + 6 * kvw:]
    pad = jnp.zeros((D, 3840 - (o + 6 * kvw + gates.shape[1])), w.dtype)
    return jnp.concatenate([q, ksl, kw, vsl, vw, kc, vc, gates, pad], axis=1).astype(BF16)


def _overlap_t(S):
    ncp = S // CMP_STRIDE
    ns = S // SLC_LEN
    cst = np.arange(ncp)[None, :] * CMP_STRIDE
    sst = np.arange(ns)[:, None] * SLC_LEN
    ov = np.clip(np.minimum(cst + CMP_LEN, sst + SLC_LEN) - np.maximum(cst, sst), 0, None)
    return jnp.asarray(ov.astype(np.float32) / CMP_LEN, dtype=BF16)


def _peer_layer(x2, mod, g, wq, subkeys, u_tab, v_tab, B, S):
    D = x2.shape[1]
    shift, scale, gate = mod[:, :D], mod[:, D:2 * D], mod[:, 2 * D:]
    q, h = norm_mod_matmul(x2, g, scale, shift, wq.astype(BF16), S, emit_h=True, out_dtype=BF16)
    route = peer_route(q, subkeys)
    E = v_tab.shape[0]
    vt = v_tab.astype(BF16).reshape(E // PEER_MIX_TE, PEER_MIX_TE, D).transpose(0, 2, 1)
    return peer_mix(h.T, u_tab.astype(BF16), vt, route, x2, gate, S)


def _even_layer(x2, mod, g, w_in, w_out, lam_vecs, subln_g, sinks, lam_init, cos64, sin64, B, S):
    D = x2.shape[1]
    shift, scale, gate = mod[:, :D], mod[:, D:2 * D], mod[:, 2 * D:]
    y = norm_mod_matmul(x2, g, scale, shift, _even_w_in_layout(w_in), S)
    qk = rope_cols(y, 0, 3584, cos64, sin64, 64, S, cw=1792)
    v = cast_cols(y, 3584, 1536, BF16, S)
    oa = diff_attention(qk, v, lam_vecs, subln_g, B, S, lam_init, q_col0=0, k_col0=1024, v_col0=0)
    ob = swa_attention(qk, v, sinks, B, S, q_col0=2048, k_col0=3072, v_col0=1024)
    o = jnp.concatenate([oa, ob], axis=1)
    return matmul_residual(o, w_out.astype(BF16), x2, gate, S)


def _odd_layer(x2, mod, g, w_in, w_out, cmp_pos, cmp_w1, cmp_w2, cos128, sin128, cos_c, sin_c, B, S):
    D = x2.shape[1]
    shift, scale, gate = mod[:, :D], mod[:, D:2 * D], mod[:, 2 * D:]
    y = norm_mod_matmul(x2, g, scale, shift, _odd_w_in_layout(w_in), S, tn=1280)
    qk = rope_cols(y, 0, 2560, cos128, sin128, 128, S, cw=1280)
    v = cast_cols(y, 2560, 512, BF16, S)
    xc = cast_cols(y, 3072, 512, F32, S)
    gates = cast_cols(y, 3584, 128, F32, S, cw=128)
    kvc = nsa_compress(xc, cmp_pos, cmp_w1, cmp_w2, cos_c, sin_c, B, S)
    oc, sel = nsa_compressed(qk, kvc, _overlap_t(S), B, S)
    os_ = nsa_selected(qk, qk, v, sel, B, S, k_col0=2048, v_col0=0)
    ow = nsa_window(qk, qk, v, B, S, k_col0=2304, v_col0=256)
    o = nsa_mix(gates, oc, os_, ow)
    return matmul_residual(o, w_out.astype(BF16), x2, gate, S)


def kernel(x, c, ada_w, ada_b, norm_g, even_w_in, even_w_out, da_lambda, da_subln, sw_sinks, odd_w_in, odd_w_out, nsa_cmp_pos, nsa_cmp_w1, nsa_cmp_w2, peer_wq, peer_subkeys, peer_u, peer_v, final_g):
    B, S, D = x.shape
    depth = ada_w.shape[0]
    pos = jnp.arange(S)
    cos64, sin64 = _rope_tables(pos, DA_DIM)
    cos128, sin128 = _rope_tables(pos, NSA_DIM)
    cos_c, sin_c = _rope_tables(jnp.arange(S // CMP_STRIDE) * CMP_STRIDE + CMP_LEN - 1, NSA_DIM)
    mods = ada_mod(c, ada_w, ada_b)
    x2 = x.reshape(B * S, D)
    for layer in range(depth):
        j = layer // 2
        if layer % 2 == 0:
            lam_init = 0.8 - 0.6 * math.exp(-0.3 * layer)
            x2 = _even_layer(x2, mods[2 * layer], norm_g[layer, 0], even_w_in[j], even_w_out[j],
                             da_lambda[j], da_subln[j], sw_sinks[j], lam_init, cos64, sin64, B, S)
        else:
            x2 = _odd_layer(x2, mods[2 * layer], norm_g[layer, 0], odd_w_in[j], odd_w_out[j],
                            nsa_cmp_pos[j], nsa_cmp_w1[j], nsa_cmp_w2[j], cos128, sin128, cos_c, sin_c, B, S)
        x2 = _peer_layer(x2, mods[2 * layer + 1], norm_g[layer, 1], peer_wq[layer], peer_subkeys[layer],
                         peer_u[layer], peer_v[layer], B, S)
    return rmsnorm_rows(x2, final_g).reshape(B, S, D)
```
